```python
import jax, jax.numpy as jnp
from jax import lax
import numpy as np

D_MODEL = 2048
BATCH = 8
SEQ = 2048
DEPTH = 1

MIX_W = D_MODEL
DN_HEADS = 8
DN_HEAD_DIM = MIX_W // (2 * DN_HEADS)
DN_W = DN_HEADS * DN_HEAD_DIM
SB_HEADS = 8
SB_HEAD_DIM = MIX_W // (2 * SB_HEADS)
SB_W = SB_HEADS * SB_HEAD_DIM
CONV_K = 4
DN_CHUNK = 64
SB_BLOCK = 128
D_FF = ((8 * D_MODEL // 3 + 255) // 256) * 256
PLE_DIM = 256
RMS_EPS = 1e-6
L2_EPS = 1e-6
IN_SIZES = (DN_W, DN_W, DN_W, DN_W, DN_HEADS, DN_HEADS, SB_W, SB_W, SB_W)
IN_COLS = sum(IN_SIZES)
IN_SPLITS = tuple(int(s) for s in np.cumsum(IN_SIZES)[:-1])

kernel_name = "hybrid_deltanet_stickbreaking_macaron_block"


def rms_norm(x, w):
    xf = x.astype(jnp.float32)
    y = xf * lax.rsqrt(jnp.mean(xf * xf, axis=-1, keepdims=True) + RMS_EPS)
    return (y * w.astype(jnp.float32)).astype(x.dtype)


def l2_norm(x):
    xf = x.astype(jnp.float32)
    return xf * lax.rsqrt(jnp.sum(xf * xf, axis=-1, keepdims=True) + L2_EPS)


def swiglu(x, w_gu, w_down):
    gate, up = jnp.split(x @ w_gu, 2, axis=-1)
    return (jax.nn.silu(gate) * up) @ w_down


def causal_depthwise_conv(x, w):
    k_w, c = w.shape
    return lax.conv_general_dilated(
        x, w[:, None, :].astype(x.dtype), window_strides=(1,),
        padding=((k_w - 1, 0),), dimension_numbers=("NWC", "WIO", "NWC"),
        feature_group_count=c)


def chunk_gated_delta_rule(q, k, v, g, beta):
    b, t, h, dk = q.shape
    dv = v.shape[-1]
    c = DN_CHUNK
    n = t // c
    to_chunks = lambda a: a.reshape(b, n, c, h, a.shape[-1]).transpose(0, 1, 3, 2, 4)
    q, k, v = to_chunks(q), to_chunks(k), to_chunks(v.astype(jnp.float32))
    g = g.astype(jnp.float32).reshape(b, n, c, h).transpose(0, 1, 3, 2)
    beta = beta.astype(jnp.float32).reshape(b, n, c, h).transpose(0, 1, 3, 2)
    g = jnp.cumsum(g, axis=-1)
    idx = jnp.arange(c)
    lower_incl = idx[:, None] >= idx[None, :]
    strict = idx[:, None] > idx[None, :]
    decay = jnp.exp(jnp.where(lower_incl, g[..., :, None] - g[..., None, :], -jnp.inf))
    k_beta = k * beta[..., None]
    v_beta = v * beta[..., None]
    lmat = jnp.where(strict, jnp.einsum("bnhid,bnhjd->bnhij", k_beta, k) * decay, 0.0)
    mmat = lmat + jnp.eye(c, dtype=jnp.float32)
    rhs = jnp.concatenate([v_beta, k_beta * jnp.exp(g)[..., None]], axis=-1)
    sol = lax.linalg.triangular_solve(mmat, rhs, left_side=True, lower=True,
                                      unit_diagonal=True)
    u, w = sol[..., :dv], sol[..., dv:]
    attn_intra = jnp.where(lower_incl, jnp.einsum("bnhid,bnhjd->bnhij", q, k) * decay, 0.0)

    def step(state, xs):
        qi, ki, ui, wi, gi, ai = xs
        v_new = ui - jnp.einsum("bhcd,bhde->bhce", wi, state)
        o = (jnp.einsum("bhcd,bhde->bhce", qi * jnp.exp(gi)[..., None], state)
             + jnp.einsum("bhij,bhje->bhie", ai, v_new))
        g_last = gi[..., -1]
        state = (state * jnp.exp(g_last)[..., None, None]
                 + jnp.einsum("bhcd,bhce->bhde", ki * jnp.exp(g_last[..., None] - gi)[..., None], v_new))
        return state, o

    xs = tuple(jnp.moveaxis(a, 1, 0) for a in (q, k, u, w, g, attn_intra))
    state0 = jnp.zeros((b, h, dk, dv), jnp.float32)
    _, o = lax.scan(step, state0, xs)
    return o.transpose(1, 0, 3, 2, 4).reshape(b, t, h, dv)


def stick_breaking_attention(q, k, v):
    b, t, h, d = q.shape
    scale = d ** -0.5
    outs = []
    for blk in range(t // SB_BLOCK):
        s0, s1 = blk * SB_BLOCK, (blk + 1) * SB_BLOCK
        z = jnp.einsum("bqhd,bkhd->bhqk", q[:, s0:s1], k[:, :s1]).astype(jnp.float32) * scale
        qpos = s0 + jnp.arange(SB_BLOCK)
        kpos = jnp.arange(s1)
        causal = kpos[None, :] < qpos[:, None]
        log_beta = jax.nn.log_sigmoid(z)
        log_1m = jnp.where(causal, jax.nn.log_sigmoid(-z), 0.0)
        rev = lax.cumsum(log_1m, axis=3, reverse=True)
        log_a = log_beta + rev - log_1m
        a = jnp.where(causal, jnp.exp(log_a), 0.0)
        outs.append(jnp.einsum("bhqk,bkhd->bqhd", a.astype(v.dtype), v[:, :s1]))
    return jnp.concatenate(outs, axis=1)


def hybrid_mixer(n, w_in, conv_w, a_log, dt_bias, dn_out_norm, w_out):
    b, t, _ = n.shape
    proj = n @ w_in
    q_dn, k_dn, v_dn, z_dn, a_dn, b_dn, q_sb, k_sb, v_sb = jnp.split(proj, IN_SPLITS, axis=-1)
    qkv = jax.nn.silu(causal_depthwise_conv(proj[..., :3 * DN_W], conv_w))
    q_c, k_c, v_c = jnp.split(qkv, 3, axis=-1)
    q_c = l2_norm(q_c.reshape(b, t, DN_HEADS, DN_HEAD_DIM)) * (DN_HEAD_DIM ** -0.5)
    k_c = l2_norm(k_c.reshape(b, t, DN_HEADS, DN_HEAD_DIM))
    v_c = v_c.reshape(b, t, DN_HEADS, DN_HEAD_DIM)
    g = -jnp.exp(a_log.astype(jnp.float32)) * jax.nn.softplus(
        a_dn.astype(jnp.float32) + dt_bias.astype(jnp.float32))
    beta = jax.nn.sigmoid(b_dn.astype(jnp.float32))
    o_dn = chunk_gated_delta_rule(q_c, k_c, v_c, g, beta).astype(n.dtype)
    o_dn = rms_norm(o_dn, dn_out_norm) * jax.nn.silu(z_dn.reshape(b, t, DN_HEADS, DN_HEAD_DIM))
    o_sb = stick_breaking_attention(q_sb.reshape(b, t, SB_HEADS, SB_HEAD_DIM),
                                    k_sb.reshape(b, t, SB_HEADS, SB_HEAD_DIM),
                                    v_sb.reshape(b, t, SB_HEADS, SB_HEAD_DIM))
    o = jnp.concatenate([o_dn.reshape(b, t, DN_W), o_sb.reshape(b, t, SB_W)], axis=-1)
    return o @ w_out


def setup_inputs(seed: int = 0) -> dict:
    key = jax.random.key(seed)
    ks = jax.random.split(key, 24)
    f32 = jnp.float32
    nrm = lambda k, shape, fan_in: jax.random.normal(k, shape, f32) * (fan_in ** -0.5)
    gain = lambda k, shape: 1.0 + 0.02 * jax.random.normal(k, shape, f32)
    dt = jnp.exp(jax.random.uniform(ks[20], (DEPTH, DN_HEADS), f32, np.log(1e-3), np.log(1e-1)))
    return {
        "x": jax.random.normal(ks[0], (BATCH, SEQ, D_MODEL), f32),
        "p": jax.random.normal(ks[1], (DEPTH, BATCH, SEQ, PLE_DIM), f32),
        "ffn1_norm": gain(ks[2], (DEPTH, D_MODEL)),
        "ffn1_w_gu": nrm(ks[3], (DEPTH, D_MODEL, 2 * D_FF), D_MODEL),
        "ffn1_w_down": nrm(ks[4], (DEPTH, D_FF, D_MODEL), D_FF),
        "mix_norm": gain(ks[5], (DEPTH, D_MODEL)),
        "w_in": nrm(ks[6], (DEPTH, D_MODEL, IN_COLS), D_MODEL),
        "dn_conv": nrm(ks[7], (DEPTH, CONV_K, 3 * DN_W), CONV_K),
        "dn_a_log": jnp.log(jax.random.uniform(ks[8], (DEPTH, DN_HEADS), f32, 1.0, 16.0)),
        "dn_dt_bias": dt + jnp.log(-jnp.expm1(-dt)),
        "dn_out_norm": gain(ks[9], (DEPTH, DN_HEAD_DIM)),
        "w_out": nrm(ks[10], (DEPTH, MIX_W, D_MODEL), MIX_W),
        "ffn2_norm": gain(ks[11], (DEPTH, D_MODEL)),
        "ffn2_w_gu": nrm(ks[12], (DEPTH, D_MODEL, 2 * D_FF), D_MODEL),
        "ffn2_w_down": nrm(ks[13], (DEPTH, D_FF, D_MODEL), D_FF),
        "ple_norm": gain(ks[14], (DEPTH, D_MODEL)),
        "ple_w_gate": nrm(ks[15], (DEPTH, D_MODEL, D_MODEL), D_MODEL),
        "ple_w_proj": nrm(ks[16], (DEPTH, PLE_DIM, D_MODEL), PLE_DIM),
        "final_norm": gain(ks[17], (D_MODEL,)),
    }


def reference(x, p, ffn1_norm, ffn1_w_gu, ffn1_w_down, mix_norm, w_in, dn_conv, dn_a_log,
              dn_dt_bias, dn_out_norm, w_out, ffn2_norm, ffn2_w_gu, ffn2_w_down,
              ple_norm, ple_w_gate, ple_w_proj, final_norm):
    h = x
    for i in range(DEPTH):
        h = h + 0.5 * swiglu(rms_norm(h, ffn1_norm[i]), ffn1_w_gu[i], ffn1_w_down[i])
        h = h + hybrid_mixer(rms_norm(h, mix_norm[i]), w_in[i], dn_conv[i], dn_a_log[i],
                             dn_dt_bias[i], dn_out_norm[i], w_out[i])
        h = h + 0.5 * swiglu(rms_norm(h, ffn2_norm[i]), ffn2_w_gu[i], ffn2_w_down[i])
        gate = jax.nn.sigmoid(rms_norm(h, ple_norm[i]) @ ple_w_gate[i])
        h = h + gate * (p[i] @ ple_w_proj[i])
    return rms_norm(h, final_norm)
```

```python
import functools

import jax
import jax.numpy as jnp
from jax import lax
from jax.experimental import pallas as pl
from jax.experimental.pallas import tpu as pltpu

_F32 = jnp.float32
_BF16 = jnp.bfloat16

RMS_EPS = 1e-6
L2_EPS = 1e-6
DN_HEADS = 8
SB_HEADS = 8
HEAD_DIM = 128
DN_CHUNK = 64
SB_BLOCK = 128
CONV_K = 4
GATE_LANES = 128
CUMSUM_PAD = 32
CONV_PAD = 8
VMEM_LIMIT = 56 * 1024 * 1024


def _dot(a, b):
    return jnp.dot(a, b, preferred_element_type=_F32)


def _dot_nt(a, b):
    return lax.dot_general(a, b, (((1,), (1,)), ((), ())), preferred_element_type=_F32)


def _dot_tn(a, b):
    return lax.dot_general(a, b, (((0,), (0,)), ((), ())), preferred_element_type=_F32)


def _dot_f32(a, b):
    return jnp.dot(a, b, preferred_element_type=_F32, precision=lax.Precision.HIGHEST)


def _rms(x, w):
    return x * lax.rsqrt(jnp.mean(x * x, axis=-1, keepdims=True) + RMS_EPS) * w


def _softplus(x):
    return jnp.maximum(x, 0.0) + jnp.log1p(jnp.exp(-jnp.abs(x)))


def _params(*sem):
    return pltpu.CompilerParams(dimension_semantics=sem, vmem_limit_bytes=VMEM_LIMIT)


def _ffn_kernel(x_ref, nw_ref, wg_ref, wu_ref, wd_ref, o_ref, n_ref):
    @pl.when(pl.program_id(1) == 0)
    def _():
        x = x_ref[...]
        n_ref[...] = _rms(x, nw_ref[...]).astype(_BF16)
        o_ref[...] = x

    n = n_ref[...]
    g = _dot(n, wg_ref[...])
    u = _dot(n, wu_ref[...])
    a = (g * jax.nn.sigmoid(g) * u).astype(_BF16)
    o_ref[...] += 0.5 * _dot(a, wd_ref[...])


def _ffn(h, norm_w, w_gu, w_down, *, tm, tf):
    m, d = h.shape
    f = w_down.shape[0]
    nf = f // tf
    return pl.pallas_call(
        _ffn_kernel,
        grid=(m // tm, nf),
        in_specs=[
            pl.BlockSpec((tm, d), lambda i, j: (i, 0)),
            pl.BlockSpec((1, d), lambda i, j: (0, 0)),
            pl.BlockSpec((d, tf), lambda i, j: (0, j)),
            pl.BlockSpec((d, tf), lambda i, j: (0, nf + j)),
            pl.BlockSpec((tf, d), lambda i, j: (j, 0)),
        ],
        out_specs=pl.BlockSpec((tm, d), lambda i, j: (i, 0)),
        out_shape=jax.ShapeDtypeStruct((m, d), _F32),
        scratch_shapes=[pltpu.VMEM((tm, d), _BF16)],
        compiler_params=_params("parallel", "arbitrary"),
    )(h, norm_w, w_gu, w_gu, w_down)


def _inproj_kernel(x_ref, nw_ref, w_ref, wgh_ref, wgl_ref, p_ref, gate_ref, n_ref):
    @pl.when(pl.program_id(1) == 0)
    def _():
        n = _rms(x_ref[...], nw_ref[...])
        n_hi = n.astype(_BF16)
        n_lo = (n - n_hi.astype(_F32)).astype(_BF16)
        n_ref[...] = n_hi
        gate_ref[...] = (_dot(n_hi, wgh_ref[...]) + _dot(n_lo, wgh_ref[...])
                         + _dot(n_hi, wgl_ref[...]))

    p_ref[...] = _dot(n_ref[...], w_ref[...]).astype(_BF16)


def _inproj(h, norm_w, w_main, w_gate_hi, w_gate_lo, *, tm, tn):
    m, d = h.shape
    n_main = w_main.shape[1]
    return pl.pallas_call(
        _inproj_kernel,
        grid=(m // tm, n_main // tn),
        in_specs=[
            pl.BlockSpec((tm, d), lambda i, j: (i, 0)),
            pl.BlockSpec((1, d), lambda i, j: (0, 0)),
            pl.BlockSpec((d, tn), lambda i, j: (0, j)),
            pl.BlockSpec((d, GATE_LANES), lambda i, j: (0, 0)),
            pl.BlockSpec((d, GATE_LANES), lambda i, j: (0, 0)),
        ],
        out_specs=[
            pl.BlockSpec((tm, tn), lambda i, j: (i, j)),
            pl.BlockSpec((tm, GATE_LANES), lambda i, j: (i, 0)),
        ],
        out_shape=[
            jax.ShapeDtypeStruct((m, n_main), _BF16),
            jax.ShapeDtypeStruct((m, GATE_LANES), _F32),
        ],
        scratch_shapes=[pltpu.VMEM((tm, d), _BF16)],
        compiler_params=_params("parallel", "arbitrary"),
    )(h, norm_w, w_main, w_gate_hi, w_gate_lo)


def _dn_kernel(q_ref, k_ref, v_ref, z_ref, gate_ref, cq_ref, ck_ref, cv_ref, gpar_ref, onorm_ref,
               o_ref, xp_ref, qn_ref, kn_ref, vc_ref, gcb_ref, bb_ref, cs_ref):
    t = q_ref.shape[0]
    c = DN_CHUNK
    head = pl.program_id(1)

    xp_ref[0:CONV_PAD, :] = jnp.zeros((CONV_PAD, HEAD_DIM), _F32)

    def conv_silu(x_ref, w_ref):
        xp_ref[CONV_PAD:CONV_PAD + t, :] = x_ref[...].astype(_F32)
        w = w_ref[...]
        acc = w[CONV_K - 1:CONV_K, :] * xp_ref[CONV_PAD:CONV_PAD + t, :]
        for s in range(1, CONV_K):
            acc = acc + w[CONV_K - 1 - s:CONV_K - s, :] * xp_ref[CONV_PAD - s:CONV_PAD - s + t, :]
        return acc * jax.nn.sigmoid(acc)

    def l2n(x):
        return x * lax.rsqrt(jnp.sum(x * x, axis=-1, keepdims=True) + L2_EPS)

    qn_ref[...] = l2n(conv_silu(q_ref, cq_ref)) * (HEAD_DIM ** -0.5)
    kn_ref[...] = l2n(conv_silu(k_ref, ck_ref))
    vc_ref[...] = conv_silu(v_ref, cv_ref)

    gates = gate_ref[...]
    gpar = gpar_ref[...]
    g_all = -jnp.exp(gpar[0:1, :]) * _softplus(gates + gpar[1:2, :])
    beta_all = jax.nn.sigmoid(gates)
    lane = lax.broadcasted_iota(jnp.int32, gates.shape, 1)
    g_col = jnp.sum(jnp.where(lane == head, g_all, 0.0), axis=-1, keepdims=True)
    beta_col = jnp.sum(jnp.where(lane == head + DN_HEADS, beta_all, 0.0), axis=-1, keepdims=True)
    bb_ref[...] = jnp.broadcast_to(beta_col, (t, HEAD_DIM))

    pos = lax.broadcasted_iota(jnp.int32, (t, HEAD_DIM), 0) & (c - 1)
    cs_ref[0:CUMSUM_PAD, :] = jnp.zeros((CUMSUM_PAD, HEAD_DIM), _F32)
    x = jnp.broadcast_to(g_col, (t, HEAD_DIM))
    s = 1
    while s < c:
        cs_ref[CUMSUM_PAD:CUMSUM_PAD + t, :] = x
        x = x + jnp.where(pos >= s, cs_ref[CUMSUM_PAD - s:CUMSUM_PAD - s + t, :], 0.0)
        s *= 2
    gcb_ref[...] = x

    row = lax.broadcasted_iota(jnp.int32, (c, c), 0)
    col = lax.broadcasted_iota(jnp.int32, (c, c), 1)
    eye = row == col
    lower_incl = row >= col
    strict = row > col
    onorm = onorm_ref[...]

    def chunk(ci, state):
        r = pl.multiple_of(ci * c, c)
        q = qn_ref[pl.ds(r, c), :]
        k = kn_ref[pl.ds(r, c), :]
        v = vc_ref[pl.ds(r, c), :]
        bb = bb_ref[pl.ds(r, c), :]
        gcb = gcb_ref[pl.ds(r, c), :]
        gc_sq = gcb[:, :c]
        gc_row = jnp.sum(jnp.where(eye, gc_sq, 0.0), axis=0, keepdims=True)
        decay = jnp.exp(jnp.where(lower_incl, gc_sq - gc_row, -jnp.inf))
        kb = k * bb
        vb = v * bb
        k16 = k.astype(_BF16)
        lmat = jnp.where(strict, _dot_nt(kb.astype(_BF16), k16) * decay, 0.0)
        attn = jnp.where(lower_incl, _dot_nt(q.astype(_BF16), k16) * decay, 0.0)
        eg = jnp.exp(gcb)
        rhs = jnp.concatenate([vb, kb * eg], axis=-1).astype(_BF16)

        tinv = jnp.where(eye, 1.0, 0.0) - lmat
        pw = lmat
        n = 2
        while n < c:
            pw = _dot_f32(pw, pw)
            tinv = tinv + _dot_f32(tinv, pw)
            n *= 2
        sol = _dot(tinv.astype(_BF16), rhs)
        u = sol[:, :HEAD_DIM]
        w = sol[:, HEAD_DIM:]

        s16 = state.astype(_BF16)
        v_new = u - _dot(w.astype(_BF16), s16)
        vn16 = v_new.astype(_BF16)
        o = _dot((q * eg).astype(_BF16), s16) + _dot(attn.astype(_BF16), vn16)
        g_last = gcb[c - 1:c, :]
        kd = (k * jnp.exp(g_last - gcb)).astype(_BF16)
        state = state * jnp.exp(g_last) + _dot_tn(kd, vn16)

        y = o * lax.rsqrt(jnp.mean(o * o, axis=-1, keepdims=True) + RMS_EPS) * onorm
        z = z_ref[pl.ds(r, c), :].astype(_F32)
        o_ref[pl.ds(r, c), :] = (y * (z * jax.nn.sigmoid(z))).astype(_BF16)
        return state

    lax.fori_loop(0, t // c, chunk, jnp.zeros((HEAD_DIM, HEAD_DIM), _F32))


def _deltanet(p_main, gates, conv_w, gate_par, out_norm, *, batch, seq):
    hd = HEAD_DIM
    nh = DN_HEADS
    col = lambda off: pl.BlockSpec((seq, hd), lambda b, h: (b, off + h))
    cw = lambda off: pl.BlockSpec((CONV_K, hd), lambda b, h: (0, off + h))
    return pl.pallas_call(
        _dn_kernel,
        grid=(batch, nh),
        in_specs=[
            col(0), col(nh), col(2 * nh), col(3 * nh),
            pl.BlockSpec((seq, GATE_LANES), lambda b, h: (b, 0)),
            cw(0), cw(nh), cw(2 * nh),
            pl.BlockSpec((2, GATE_LANES), lambda b, h: (0, 0)),
            pl.BlockSpec((1, hd), lambda b, h: (0, 0)),
        ],
        out_specs=pl.BlockSpec((seq, hd), lambda b, h: (b, h)),
        out_shape=jax.ShapeDtypeStruct((batch * seq, nh * hd), _BF16),
        scratch_shapes=[
            pltpu.VMEM((CONV_PAD + seq, hd), _F32),
            pltpu.VMEM((seq, hd), _F32),
            pltpu.VMEM((seq, hd), _F32),
            pltpu.VMEM((seq, hd), _F32),
            pltpu.VMEM((seq, hd), _F32),
            pltpu.VMEM((seq, hd), _F32),
            pltpu.VMEM((CUMSUM_PAD + seq, hd), _F32),
        ],
        compiler_params=_params("parallel", "arbitrary"),
    )(p_main, p_main, p_main, p_main, gates, conv_w, conv_w, conv_w, gate_par, out_norm)


def _sb_kernel(q_ref, k_ref, v_ref, o_ref):
    blk = SB_BLOCK
    qb = pl.program_id(2)
    q = q_ref[...]
    scale = HEAD_DIM ** -0.5
    row = lax.broadcasted_iota(jnp.int32, (blk, blk), 0)
    col = lax.broadcasted_iota(jnp.int32, (blk, blk), 1)
    causal = col < row
    r2 = lax.broadcasted_iota(jnp.int32, (blk, 2 * blk), 0)
    c2 = lax.broadcasted_iota(jnp.int32, (blk, 2 * blk), 1)
    tri = jnp.where((r2 >= c2) | (c2 >= blk), 1.0, 0.0).astype(_BF16)

    def block(kb, o, carry, masked):
        ks = pl.multiple_of(kb * blk, blk)
        k = k_ref[pl.ds(ks, blk), :]
        v = v_ref[pl.ds(ks, blk), :]
        z = _dot_nt(q, k) * scale
        sp = jnp.log1p(jnp.exp(-jnp.abs(z)))
        log_beta = jnp.minimum(z, 0.0) - sp
        log_1m = -jnp.maximum(z, 0.0) - sp
        if masked:
            log_1m = jnp.where(causal, log_1m, 0.0)
        hi = log_1m.astype(_BF16)
        lo = (log_1m - hi.astype(_F32)).astype(_BF16)
        sums = _dot(hi, tri) + _dot(lo, tri)
        log_a = log_beta + (sums[:, :blk] + carry) - log_1m
        a = jnp.exp(log_a)
        if masked:
            a = jnp.where(causal, a, 0.0)
        o = o + _dot(a.astype(_BF16), v)
        return o, carry + sums[:, blk:]

    zeros = jnp.zeros((blk, HEAD_DIM), _F32)
    o, carry = block(qb, zeros, zeros, True)

    def body(i, oc):
        return block(qb - 1 - i, oc[0], oc[1], False)

    o, _ = lax.fori_loop(0, qb, body, (o, carry))
    o_ref[...] = o.astype(_BF16)


def _stickbreak(p_main, *, batch, seq, col_off):
    hd = HEAD_DIM
    nh = SB_HEADS
    nqb = seq // SB_BLOCK
    return pl.pallas_call(
        _sb_kernel,
        grid=(batch, nh, nqb),
        in_specs=[
            pl.BlockSpec((SB_BLOCK, hd), lambda b, h, i: (b * nqb + i, col_off + h)),
            pl.BlockSpec((seq, hd), lambda b, h, i: (b, col_off + nh + h)),
            pl.BlockSpec((seq, hd), lambda b, h, i: (b, col_off + 2 * nh + h)),
        ],
        out_specs=pl.BlockSpec((SB_BLOCK, hd), lambda b, h, i: (b * nqb + i, h)),
        out_shape=jax.ShapeDtypeStruct((batch * seq, nh * hd), _BF16),
        compiler_params=_params("parallel", "parallel", "arbitrary"),
    )(p_main, p_main, p_main)


def _outproj_kernel(h_ref, a_ref, b_ref, wa_ref, wb_ref, o_ref):
    o_ref[...] = h_ref[...] + _dot(a_ref[...], wa_ref[...]) + _dot(b_ref[...], wb_ref[...])


def _outproj(h, o_dn, o_sb, w_out, *, tm):
    m, d = h.shape
    ka = o_dn.shape[1]
    kb = o_sb.shape[1]
    assert ka == kb
    return pl.pallas_call(
        _outproj_kernel,
        grid=(m // tm,),
        in_specs=[
            pl.BlockSpec((tm, d), lambda i: (i, 0)),
            pl.BlockSpec((tm, ka), lambda i: (i, 0)),
            pl.BlockSpec((tm, kb), lambda i: (i, 0)),
            pl.BlockSpec((ka, d), lambda i: (0, 0)),
            pl.BlockSpec((kb, d), lambda i: (1, 0)),
        ],
        out_specs=pl.BlockSpec((tm, d), lambda i: (i, 0)),
        out_shape=jax.ShapeDtypeStruct((m, d), _F32),
        compiler_params=_params("parallel"),
    )(h, o_dn, o_sb, w_out, w_out)


def _ple_kernel(h_ref, p_ref, nw_ref, wg_ref, wp_ref, fw_ref, o_ref):
    h = h_ref[...]
    n = _rms(h, nw_ref[...]).astype(_BF16)
    gate = jax.nn.sigmoid(_dot(n, wg_ref[...]))
    h = h + gate * _dot(p_ref[...].astype(_BF16), wp_ref[...])
    o_ref[...] = _rms(h, fw_ref[...])


def _ple(h, p, norm_w, w_gate, w_proj, final_w, *, tm):
    m, d = h.shape
    pd = p.shape[1]
    return pl.pallas_call(
        _ple_kernel,
        grid=(m // tm,),
        in_specs=[
            pl.BlockSpec((tm, d), lambda i: (i, 0)),
            pl.BlockSpec((tm, pd), lambda i: (i, 0)),
            pl.BlockSpec((1, d), lambda i: (0, 0)),
            pl.BlockSpec((d, d), lambda i: (0, 0)),
            pl.BlockSpec((pd, d), lambda i: (0, 0)),
            pl.BlockSpec((1, d), lambda i: (0, 0)),
        ],
        out_specs=pl.BlockSpec((tm, d), lambda i: (i, 0)),
        out_shape=jax.ShapeDtypeStruct((m, d), _F32),
        compiler_params=_params("parallel"),
    )(h, p, norm_w, w_gate, w_proj, final_w)


def _pad_lanes(v, width):
    return jnp.pad(v.astype(_F32), (0, width - v.shape[0]))


def kernel(x, p, ffn1_norm, ffn1_w_gu, ffn1_w_down, mix_norm, w_in, dn_conv, dn_a_log, dn_dt_bias, dn_out_norm, w_out, ffn2_norm, ffn2_w_gu, ffn2_w_down, ple_norm, ple_w_gate, ple_w_proj, final_norm):
    batch, seq, d = x.shape
    depth = p.shape[0]
    m = batch * seq
    dn_w = DN_HEADS * HEAD_DIM
    sb_w = SB_HEADS * HEAD_DIM
    gate_lo = 4 * dn_w
    gate_hi = gate_lo + 2 * DN_HEADS
    row = lambda v: v.reshape(1, -1).astype(_F32)

    h = x.reshape(m, d)
    for i in range(depth):
        h = _ffn(h, row(ffn1_norm[i]), ffn1_w_gu[i].astype(_BF16), ffn1_w_down[i].astype(_BF16),
                 tm=512, tf=512)

        wi = w_in[i]
        w_main = jnp.concatenate([wi[:, :gate_lo], wi[:, gate_hi:]], axis=1).astype(_BF16)
        w_gate = jnp.pad(wi[:, gate_lo:gate_hi], ((0, 0), (0, GATE_LANES - 2 * DN_HEADS)))
        w_gate_hi = w_gate.astype(_BF16)
        w_gate_lo = (w_gate - w_gate_hi.astype(_F32)).astype(_BF16)
        p_main, gates = _inproj(h, row(mix_norm[i]), w_main, w_gate_hi, w_gate_lo, tm=512, tn=512)

        gate_par = jnp.stack([_pad_lanes(dn_a_log[i], GATE_LANES), _pad_lanes(dn_dt_bias[i], GATE_LANES)])
        o_dn = _deltanet(p_main, gates, dn_conv[i].astype(_F32), gate_par, row(dn_out_norm[i]),
                         batch=batch, seq=seq)
        o_sb = _stickbreak(p_main, batch=batch, seq=seq, col_off=4 * DN_HEADS)
        h = _outproj(h, o_dn, o_sb, w_out[i].astype(_BF16), tm=512)

        h = _ffn(h, row(ffn2_norm[i]), ffn2_w_gu[i].astype(_BF16), ffn2_w_down[i].astype(_BF16),
                 tm=512, tf=512)
        last = i == depth - 1
        assert last, "the final RMSNorm is fused into the last layer's embedding kernel"
        h = _ple(h, p[i].reshape(m, -1), row(ple_norm[i]), ple_w_gate[i].astype(_BF16),
                 ple_w_proj[i].astype(_BF16), row(final_norm), tm=512)
    return h.reshape(batch, seq, d)
```

```python
import math

import jax
import jax.numpy as jnp
from jax import lax
from jax.experimental import pallas as pl
from jax.experimental.pallas import tpu as pltpu

_F32 = jnp.float32
_BF16 = jnp.bfloat16

RMS_EPS = 1e-6
L2_EPS = 1e-6
DN_HEADS = 8
SB_HEADS = 8
HEAD_DIM = 128
CONV_K = 4
GATE_LANES = 128
CONV_PAD = 8

DN_CHUNK = 64
DN_GROUP = 4
DN_GROUP_ROWS = DN_CHUNK * DN_GROUP
DN_GROUPS_PER_ITER = 4
DN_HEADS_PER_STEP = 4
CUMSUM_PAD = DN_CHUNK // 2

SB_TQ = 256
SB_TK = 256
SB_HEADS_PER_STEP = 8

VMEM_LIMIT = 56 * 1024 * 1024
LOG2E = math.log2(math.e)


def _dot(a, b):
    return jnp.dot(a, b, preferred_element_type=_F32)


def _dot_nt(a, b):
    return lax.dot_general(a, b, (((1,), (1,)), ((), ())), preferred_element_type=_F32)


def _dot_tn(a, b):
    return lax.dot_general(a, b, (((0,), (0,)), ((), ())), preferred_element_type=_F32)


def _rms(x, w):
    return x * lax.rsqrt(jnp.mean(x * x, axis=-1, keepdims=True) + RMS_EPS) * w


def _softplus(x):
    return jnp.maximum(x, 0.0) + jnp.log1p(jnp.exp(-jnp.abs(x)))


def _silu(x):
    return x * jax.nn.sigmoid(x)


def _params(*sem):
    return pltpu.CompilerParams(dimension_semantics=sem, vmem_limit_bytes=VMEM_LIMIT)


def _ffn_kernel(x_ref, nw_ref, wg_ref, wu_ref, wd_ref, o_ref, n_ref):
    @pl.when(pl.program_id(1) == 0)
    def _():
        x = x_ref[...]
        n_ref[...] = _rms(x, nw_ref[...]).astype(_BF16)
        o_ref[...] = x

    n = n_ref[...]
    g = _dot(n, wg_ref[...])
    u = _dot(n, wu_ref[...])
    a = (_silu(g) * u).astype(_BF16)
    o_ref[...] += 0.5 * _dot(a, wd_ref[...])


def _ffn(h, norm_w, w_gu, w_down, *, tm, tf):
    m, d = h.shape
    f = w_down.shape[0]
    nf = f // tf
    return pl.pallas_call(
        _ffn_kernel,
        grid=(m // tm, nf),
        in_specs=[
            pl.BlockSpec((tm, d), lambda i, j: (i, 0)),
            pl.BlockSpec((1, d), lambda i, j: (0, 0)),
            pl.BlockSpec((d, tf), lambda i, j: (0, j)),
            pl.BlockSpec((d, tf), lambda i, j: (0, nf + j)),
            pl.BlockSpec((tf, d), lambda i, j: (j, 0)),
        ],
        out_specs=pl.BlockSpec((tm, d), lambda i, j: (i, 0)),
        out_shape=jax.ShapeDtypeStruct((m, d), _F32),
        scratch_shapes=[pltpu.VMEM((tm, d), _BF16)],
        compiler_params=_params("parallel", "arbitrary"),
        name="ffn",
    )(h, norm_w, w_gu, w_gu, w_down)


def _inproj_kernel(x_ref, nw_ref, w_ref, wg_ref, p_ref, gate_ref, n_ref):
    @pl.when(pl.program_id(1) == 0)
    def _():
        n = _rms(x_ref[...], nw_ref[...])
        n_hi = n.astype(_BF16)
        n_lo = (n - n_hi.astype(_F32)).astype(_BF16)
        n_ref[...] = n_hi
        s = _dot(n_hi, wg_ref[...]) + _dot(n_lo, wg_ref[...])
        lane = lax.broadcasted_iota(jnp.int32, s.shape, 1)
        folded = s + pltpu.roll(s, GATE_LANES - 2 * DN_HEADS, axis=1)
        gate_ref[...] = jnp.where(lane < 2 * DN_HEADS, folded, 0.0)

    res = _dot(n_ref[...], w_ref[...]).astype(_BF16)
    for c in range(p_ref.shape[0]):
        p_ref[c] = res[:, c * HEAD_DIM:(c + 1) * HEAD_DIM]


def _inproj(h, norm_w, w_main, w_gate, *, tm, tn):
    m, d = h.shape
    n_main = w_main.shape[1]
    hpt = tn // HEAD_DIM
    return pl.pallas_call(
        _inproj_kernel,
        grid=(m // tm, n_main // tn),
        in_specs=[
            pl.BlockSpec((tm, d), lambda i, j: (i, 0)),
            pl.BlockSpec((1, d), lambda i, j: (0, 0)),
            pl.BlockSpec((d, tn), lambda i, j: (0, j)),
            pl.BlockSpec((d, GATE_LANES), lambda i, j: (0, 0)),
        ],
        out_specs=[
            pl.BlockSpec((hpt, tm, HEAD_DIM), lambda i, j: (j, i, 0)),
            pl.BlockSpec((tm, GATE_LANES), lambda i, j: (i, 0)),
        ],
        out_shape=[
            jax.ShapeDtypeStruct((n_main // HEAD_DIM, m, HEAD_DIM), _BF16),
            jax.ShapeDtypeStruct((m, GATE_LANES), _F32),
        ],
        scratch_shapes=[pltpu.VMEM((tm, d), _BF16)],
        compiler_params=_params("parallel", "arbitrary"),
        name="in_proj",
    )(h, norm_w, w_main, w_gate)


def _dn_kernel(q_ref, k_ref, v_ref, z_ref, gate_ref, cq_ref, ck_ref, cv_ref, gpar_ref, onorm_ref,
               o_ref,
               xp_ref, qn_ref, kn_ref, vc_ref, gcb_ref, bb_ref, cs_ref, gall_ref, ball_ref,
               u_ref, wq_ref, a_ref, kd_ref, egl_ref):
    nh, t, hd = q_ref.shape
    c = DN_CHUNK
    gr = DN_GROUP_ROWS
    head0 = pl.program_id(1) * nh

    gates = gate_ref[...]
    gpar = gpar_ref[...]
    ball_ref[...] = jax.nn.sigmoid(gates)
    x = -jnp.exp(gpar[0:1, :]) * _softplus(gates + gpar[1:2, :])
    pos = lax.broadcasted_iota(jnp.int32, (t, GATE_LANES), 0) & (c - 1)
    cs_ref[0:CUMSUM_PAD, :] = jnp.zeros((CUMSUM_PAD, GATE_LANES), _F32)
    s = 1
    while s < c:
        cs_ref[CUMSUM_PAD:CUMSUM_PAD + t, :] = x
        x = x + jnp.where(pos >= s, cs_ref[CUMSUM_PAD - s:CUMSUM_PAD - s + t, :], 0.0)
        s *= 2
    gall_ref[...] = x
    xp_ref[0:CONV_PAD, :] = jnp.zeros((CONV_PAD, hd), _F32)

    row = lax.broadcasted_iota(jnp.int32, (gr, gr), 0)
    col = lax.broadcasted_iota(jnp.int32, (gr, gr), 1)
    same = (row // c) == (col // c)
    eye = row == col
    lower_incl = same & (row >= col)
    strict = same & (row > col)

    def conv_silu(x_ref, w):
        xp_ref[CONV_PAD:CONV_PAD + t, :] = x_ref[...].astype(_F32)
        acc = w[CONV_K - 1:CONV_K, :] * xp_ref[CONV_PAD:CONV_PAD + t, :]
        for sh in range(1, CONV_K):
            acc = acc + w[CONV_K - 1 - sh:CONV_K - sh, :] * xp_ref[CONV_PAD - sh:CONV_PAD - sh + t, :]
        return _silu(acc)

    def l2n(y):
        return y * lax.rsqrt(jnp.sum(y * y, axis=-1, keepdims=True) + L2_EPS)

    def solve_groups(rows):
        n = range(len(rows))
        q = [qn_ref[pl.ds(r, gr), :] for r in rows]
        k = [kn_ref[pl.ds(r, gr), :] for r in rows]
        v = [vc_ref[pl.ds(r, gr), :] for r in rows]
        bb = [bb_ref[pl.ds(r, gr), :] for r in rows]
        gcb = [gcb_ref[pl.ds(r, gr), :] for r in rows]
        k16 = [k[i].astype(_BF16) for i in n]
        kb = [k[i] * bb[i] for i in n]
        kk = [_dot_nt(kb[i].astype(_BF16), k16[i]) for i in n]
        qk = [_dot_nt(q[i].astype(_BF16), k16[i]) for i in n]
        eg = [jnp.exp(gcb[i]) for i in n]
        lmat, attn, sol = [], [], []
        for i in n:
            gc_sq = jnp.concatenate([gcb[i]] * (gr // hd), axis=-1)
            gc_row = jnp.sum(jnp.where(eye, gc_sq, 0.0), axis=0, keepdims=True)
            decay = jnp.exp(jnp.where(lower_incl, gc_sq - gc_row, -jnp.inf))
            lmat.append(jnp.where(strict, kk[i] * decay, 0.0).astype(_BF16))
            attn.append((qk[i] * decay).astype(_BF16))
            sol.append(jnp.concatenate([v[i] * bb[i], kb[i] * eg[i]], axis=-1))

        sol = [sol[i] - _dot(lmat[i], sol[i].astype(_BF16)) for i in n]
        pw = lmat
        m = 2
        while m < c:
            pw = [_dot(pw[i], pw[i]).astype(_BF16) for i in n]
            sol = [sol[i] + _dot(pw[i], sol[i].astype(_BF16)) for i in n]
            m *= 2

        res = []
        for i in n:
            g_last = jnp.concatenate(
                [jnp.broadcast_to(gcb[i][(j + 1) * c - 1:(j + 1) * c, :], (c, hd)) for j in range(DN_GROUP)], axis=0)
            w16 = sol[i][:, hd:].astype(_BF16)
            qg16 = (q[i] * eg[i]).astype(_BF16)
            kd16 = (k[i] * jnp.exp(g_last - gcb[i])).astype(_BF16)
            res.append((sol[i][:, :hd], w16, qg16, kd16, attn[i], eg[i]))
        return res

    def store_group(hl, r, u, w16, qg16, kd16, attn16, eg):
        u_ref[hl, pl.ds(r, gr), :] = u
        kd_ref[hl, pl.ds(r, gr), :] = kd16
        for i in range(DN_GROUP):
            lo, hi = i * c, (i + 1) * c
            r2 = pl.multiple_of(2 * r + 2 * lo, c)
            wq_ref[hl, pl.ds(r2, c), :] = w16[lo:hi]
            wq_ref[hl, pl.ds(r2 + c, c), :] = qg16[lo:hi]
            a_ref[hl, pl.ds(pl.multiple_of(r + lo, c), c), :] = attn16[lo:hi, lo:hi]
            r8 = pl.multiple_of((r + lo) // (c // 8), 8)
            egl_ref[hl, pl.ds(r8, 8), :] = jnp.broadcast_to(eg[hi - 1:hi, :], (8, hd))

    def head_prologue(hl, carry):
        lane = lax.broadcasted_iota(jnp.int32, (t, GATE_LANES), 1)
        head = head0 + hl
        g_col = jnp.sum(jnp.where(lane == head, gall_ref[...], 0.0), axis=-1, keepdims=True)
        b_col = jnp.sum(jnp.where(lane == head + DN_HEADS, ball_ref[...], 0.0), axis=-1, keepdims=True)
        gcb_ref[...] = jnp.broadcast_to(g_col, (t, hd))
        bb_ref[...] = jnp.broadcast_to(b_col, (t, hd))
        qn_ref[...] = l2n(conv_silu(q_ref.at[hl], cq_ref[hl])) * (hd ** -0.5)
        kn_ref[...] = l2n(conv_silu(k_ref.at[hl], ck_ref[hl]))
        vc_ref[...] = conv_silu(v_ref.at[hl], cv_ref[hl])

        def groups(gi, cc):
            rows = [pl.multiple_of((gi * DN_GROUPS_PER_ITER + sub) * gr, gr) for sub in range(DN_GROUPS_PER_ITER)]
            solved = solve_groups(rows)
            for r, res in zip(rows, solved):
                store_group(hl, r, *res)
            return cc

        lax.fori_loop(0, t // (gr * DN_GROUPS_PER_ITER), groups, 0)
        return carry

    lax.fori_loop(0, nh, head_prologue, 0)

    onorm = onorm_ref[...]

    def chunk(ci, states):
        r = pl.multiple_of(ci * c, c)
        r2 = pl.multiple_of(ci * 2 * c, 2 * c)
        r8 = pl.multiple_of(ci * 8, 8)
        heads = range(nh)
        wq = [wq_ref[hl, pl.ds(r2, 2 * c), :] for hl in heads]
        u = [u_ref[hl, pl.ds(r, c), :] for hl in heads]
        attn = [a_ref[hl, pl.ds(r, c), :] for hl in heads]
        kd = [kd_ref[hl, pl.ds(r, c), :] for hl in heads]
        e_last = [egl_ref[hl, pl.ds(r8, 8), :][0:1] for hl in heads]
        z = [z_ref[hl, pl.ds(r, c), :] for hl in heads]
        ws = [_dot(wq[hl], states[hl].astype(_BF16)) for hl in heads]
        v_new = [(u[hl] - ws[hl][:c]).astype(_BF16) for hl in heads]
        new_states = tuple(states[hl] * e_last[hl] + _dot_tn(kd[hl], v_new[hl]) for hl in heads)
        outs = []
        for hl in heads:
            o = ws[hl][c:] + _dot(attn[hl], v_new[hl])
            y = o * lax.rsqrt(jnp.mean(o * o, axis=-1, keepdims=True) + RMS_EPS) * onorm
            outs.append((y * _silu(z[hl].astype(_F32))).astype(_BF16))
        for hl in heads:
            o_ref[pl.ds(r, c), hl * hd:(hl + 1) * hd] = outs[hl]
        return new_states

    lax.fori_loop(0, t // c, chunk, tuple(jnp.zeros((hd, hd), _F32) for _ in range(nh)))


def _deltanet(p_heads, gates, conv_w, gate_par, out_norm, *, batch, seq):
    hd = HEAD_DIM
    nh = DN_HEADS_PER_STEP
    ng = DN_HEADS // nh
    c = DN_CHUNK
    heads = lambda part: pl.BlockSpec((nh, seq, hd), lambda b, g: (part * ng + g, b, 0))
    convw = lambda part: pl.BlockSpec((nh, CONV_K, hd), lambda b, g: (part * ng + g, 0, 0))
    f32_rows = lambda: pltpu.VMEM((seq, hd), _F32)
    return pl.pallas_call(
        _dn_kernel,
        grid=(batch, ng),
        in_specs=[
            heads(0), heads(1), heads(2), heads(3),
            pl.BlockSpec((seq, GATE_LANES), lambda b, g: (b, 0)),
            convw(0), convw(1), convw(2),
            pl.BlockSpec((2, GATE_LANES), lambda b, g: (0, 0)),
            pl.BlockSpec((1, hd), lambda b, g: (0, 0)),
        ],
        out_specs=pl.BlockSpec((seq, nh * hd), lambda b, g: (b, g)),
        out_shape=jax.ShapeDtypeStruct((batch * seq, DN_HEADS * hd), _BF16),
        scratch_shapes=[
            pltpu.VMEM((CONV_PAD + seq, hd), _F32),
            f32_rows(), f32_rows(), f32_rows(),
            f32_rows(), f32_rows(),
            pltpu.VMEM((CUMSUM_PAD + seq, GATE_LANES), _F32),
            f32_rows(), f32_rows(),
            pltpu.VMEM((nh, seq, hd), _F32),
            pltpu.VMEM((nh, 2 * seq, hd), _BF16),
            pltpu.VMEM((nh, seq, c), _BF16),
            pltpu.VMEM((nh, seq, hd), _BF16),
            pltpu.VMEM((nh, seq // c * 8, hd), _F32),
        ],
        compiler_params=_params("parallel", "arbitrary"),
        name="deltanet",
    )(p_heads, p_heads, p_heads, p_heads, gates, conv_w, conv_w, conv_w, gate_par, out_norm)


def _sb_kernel(q_ref, k_ref, v_ref, o_ref):
    nh, tq, hd = q_ref.shape
    tk = SB_TK
    qi = pl.program_id(2)
    zscale = (hd ** -0.5) * LOG2E
    row = lax.broadcasted_iota(jnp.int32, (tq, tk), 0)
    col = lax.broadcasted_iota(jnp.int32, (tq, tk), 1)
    causal = col < row
    ntri = jnp.where(row >= col, -1.0, 0.0).astype(_BF16)
    ntri2 = jnp.concatenate([ntri, ntri], axis=0)

    def tile(kt, state, masked):
        ks = pl.multiple_of(kt * tk, tk)
        heads = range(nh)
        ks_ = [k_ref[h, pl.ds(ks, tk), :] for h in heads]
        vs_ = [v_ref[h, pl.ds(ks, tk), :] for h in heads]
        z2 = [_dot_nt(q_ref[h], ks_[h]) * zscale for h in heads]
        revs = []
        for h in heads:
            neg_abs = pltpu.bitcast(pltpu.bitcast(z2[h], jnp.uint32) | jnp.uint32(0x80000000), _F32)
            x = jnp.maximum(z2[h], 0.0) + jnp.log2(1.0 + jnp.exp2(neg_abs))
            if masked:
                x = jnp.where(causal, x, 0.0)
            hi = x.astype(_BF16)
            lo = (x - hi.astype(_F32)).astype(_BF16)
            revs.append(_dot(jnp.concatenate([hi, lo], axis=-1), ntri2))
        out = []
        for h in heads:
            acc, carry = state[2 * h], state[2 * h + 1]
            a = jnp.exp2(z2[h] + revs[h] + jnp.concatenate([carry] * (tk // hd), axis=-1))
            if masked:
                a = jnp.where(causal, a, 0.0)
            out.append(acc + _dot(a.astype(_BF16), vs_[h]))
            out.append(carry + jnp.broadcast_to(revs[h][:, 0:1], (tq, hd)))
        return tuple(out)

    zeros = jnp.zeros((tq, hd), _F32)
    state = tile(qi, (zeros,) * (2 * nh), True)
    state = lax.fori_loop(0, qi, lambda i, st: tile(qi - 1 - i, st, False), state)
    for h in range(nh):
        o_ref[:, h * hd:(h + 1) * hd] = state[2 * h].astype(_BF16)


def _stickbreak(p_heads, *, batch, seq, head_off):
    hd = HEAD_DIM
    nh = SB_HEADS_PER_STEP
    ng = SB_HEADS // nh
    nq = seq // SB_TQ
    assert SB_TQ == SB_TK
    base = head_off // nh
    return pl.pallas_call(
        _sb_kernel,
        grid=(batch, ng, nq),
        in_specs=[
            pl.BlockSpec((nh, SB_TQ, hd), lambda b, g, i: (base + g, b * nq + i, 0)),
            pl.BlockSpec((nh, seq, hd), lambda b, g, i: (base + ng + g, b, 0)),
            pl.BlockSpec((nh, seq, hd), lambda b, g, i: (base + 2 * ng + g, b, 0)),
        ],
        out_specs=pl.BlockSpec((SB_TQ, nh * hd), lambda b, g, i: (b * nq + i, g)),
        out_shape=jax.ShapeDtypeStruct((batch * seq, SB_HEADS * hd), _BF16),
        compiler_params=_params("parallel", "parallel", "arbitrary"),
        name="stickbreak",
    )(p_heads, p_heads, p_heads)


def _outproj_kernel(h_ref, a_ref, b_ref, wa_ref, wb_ref, o_ref):
    o_ref[...] = h_ref[...] + _dot(a_ref[...], wa_ref[...]) + _dot(b_ref[...], wb_ref[...])


def _outproj(h, o_dn, o_sb, w_out, *, tm):
    m, d = h.shape
    ka = o_dn.shape[1]
    kb = o_sb.shape[1]
    assert ka == kb
    return pl.pallas_call(
        _outproj_kernel,
        grid=(m // tm,),
        in_specs=[
            pl.BlockSpec((tm, d), lambda i: (i, 0)),
            pl.BlockSpec((tm, ka), lambda i: (i, 0)),
            pl.BlockSpec((tm, kb), lambda i: (i, 0)),
            pl.BlockSpec((ka, d), lambda i: (0, 0)),
            pl.BlockSpec((kb, d), lambda i: (1, 0)),
        ],
        out_specs=pl.BlockSpec((tm, d), lambda i: (i, 0)),
        out_shape=jax.ShapeDtypeStruct((m, d), _F32),
        compiler_params=_params("parallel"),
        name="out_proj",
    )(h, o_dn, o_sb, w_out, w_out)


def _ple_kernel(h_ref, p_ref, nw_ref, wg_ref, wp_ref, fw_ref, o_ref):
    h = h_ref[...]
    n = _rms(h, nw_ref[...]).astype(_BF16)
    gate = jax.nn.sigmoid(_dot(n, wg_ref[...]))
    h = h + gate * _dot(p_ref[...].astype(_BF16), wp_ref[...])
    o_ref[...] = _rms(h, fw_ref[...])


def _ple(h, p, norm_w, w_gate, w_proj, final_w, *, tm):
    m, d = h.shape
    pd = p.shape[1]
    return pl.pallas_call(
        _ple_kernel,
        grid=(m // tm,),
        in_specs=[
            pl.BlockSpec((tm, d), lambda i: (i, 0)),
            pl.BlockSpec((tm, pd), lambda i: (i, 0)),
            pl.BlockSpec((1, d), lambda i: (0, 0)),
            pl.BlockSpec((d, d), lambda i: (0, 0)),
            pl.BlockSpec((pd, d), lambda i: (0, 0)),
            pl.BlockSpec((1, d), lambda i: (0, 0)),
        ],
        out_specs=pl.BlockSpec((tm, d), lambda i: (i, 0)),
        out_shape=jax.ShapeDtypeStruct((m, d), _F32),
        compiler_params=_params("parallel"),
        name="ple_final",
    )(h, p, norm_w, w_gate, w_proj, final_w)


def _pad_lanes(v, width):
    return jnp.pad(v.astype(_F32), (0, width - v.shape[0]))


def kernel(x, p, ffn1_norm, ffn1_w_gu, ffn1_w_down, mix_norm, w_in, dn_conv, dn_a_log, dn_dt_bias, dn_out_norm, w_out, ffn2_norm, ffn2_w_gu, ffn2_w_down, ple_norm, ple_w_gate, ple_w_proj, final_norm):
    batch, seq, d = x.shape
    depth = p.shape[0]
    m = batch * seq
    dn_w = DN_HEADS * HEAD_DIM
    gate_lo = 4 * dn_w
    gate_hi = gate_lo + 2 * DN_HEADS
    row = lambda v: v.reshape(1, -1).astype(_F32)

    h = x.reshape(m, d)
    for i in range(depth):
        h = _ffn(h, row(ffn1_norm[i]), ffn1_w_gu[i].astype(_BF16), ffn1_w_down[i].astype(_BF16),
                 tm=512, tf=512)

        wi = w_in[i]
        w_main = jnp.concatenate([wi[:, :gate_lo], wi[:, gate_hi:]], axis=1).astype(_BF16)
        w_gate = wi[:, gate_lo:gate_hi]
        w_gate_hi = w_gate.astype(_BF16)
        w_gate_lo = (w_gate - w_gate_hi.astype(_F32)).astype(_BF16)
        w_gate = jnp.pad(jnp.concatenate([w_gate_hi, w_gate_lo], axis=1), ((0, 0), (0, GATE_LANES - 4 * DN_HEADS)))
        p_heads, gates = _inproj(h, row(mix_norm[i]), w_main, w_gate, tm=1024, tn=1024)

        gate_par = jnp.stack([_pad_lanes(dn_a_log[i], GATE_LANES), _pad_lanes(dn_dt_bias[i], GATE_LANES)])
        conv_w = dn_conv[i].astype(_F32).reshape(CONV_K, 3 * DN_HEADS, HEAD_DIM).transpose(1, 0, 2)
        o_dn = _deltanet(p_heads, gates, conv_w, gate_par, row(dn_out_norm[i]), batch=batch, seq=seq)
        o_sb = _stickbreak(p_heads, batch=batch, seq=seq, head_off=4 * DN_HEADS)
        h = _outproj(h, o_dn, o_sb, w_out[i].astype(_BF16), tm=512)

        h = _ffn(h, row(ffn2_norm[i]), ffn2_w_gu[i].astype(_BF16), ffn2_w_down[i].astype(_BF16),
                 tm=512, tf=512)
        assert i == depth - 1, "the final RMSNorm is fused into the last layer's embedding kernel"
        h = _ple(h, p[i].reshape(m, -1), row(ple_norm[i]), ple_w_gate[i].astype(_BF16),
                 ple_w_proj[i].astype(_BF16), row(final_norm), tm=512)
    return h.reshape(batch, seq, d)
```

```python
import math

import jax
import jax.numpy as jnp
from jax import lax
from jax.experimental import pallas as pl
from jax.experimental.pallas import tpu as pltpu

_F32 = jnp.float32
_BF16 = jnp.bfloat16

RMS_EPS = 1e-6
L2_EPS = 1e-6
DN_HEADS = 8
SB_HEADS = 8
HEAD_DIM = 128
CONV_K = 4
GATE_LANES = 128
CONV_PAD = 8

DN_CHUNK = 64
DN_GROUP = 4
DN_GROUP_ROWS = DN_CHUNK * DN_GROUP
DN_GROUPS_PER_ITER = 8
DN_HEADS_PER_STEP = 4
CUMSUM_PAD = DN_CHUNK // 2

SB_TQ = 256
SB_TK = 256
SB_HEADS_PER_STEP = 8

VMEM_LIMIT = 56 * 1024 * 1024
LOG2E = math.log2(math.e)


def _dot(a, b):
    return jnp.dot(a, b, preferred_element_type=_F32)


def _dot_nt(a, b):
    return lax.dot_general(a, b, (((1,), (1,)), ((), ())), preferred_element_type=_F32)


def _dot_tn(a, b):
    return lax.dot_general(a, b, (((0,), (0,)), ((), ())), preferred_element_type=_F32)


def _rms(x, w):
    return x * lax.rsqrt(jnp.mean(x * x, axis=-1, keepdims=True) + RMS_EPS) * w


def _softplus(x):
    return jnp.maximum(x, 0.0) + jnp.log1p(jnp.exp(-jnp.abs(x)))


def _silu(x):
    return x * jax.nn.sigmoid(x)


def _params(*sem):
    return pltpu.CompilerParams(dimension_semantics=sem, vmem_limit_bytes=VMEM_LIMIT)


def _ffn_kernel(x_ref, nw_ref, wg_ref, wu_ref, wd_ref, o_ref, n_ref):
    @pl.when(pl.program_id(1) == 0)
    def _():
        x = x_ref[...]
        n_ref[...] = _rms(x, nw_ref[...]).astype(_BF16)
        o_ref[...] = x

    n = n_ref[...]
    g = _dot(n, wg_ref[...])
    u = _dot(n, wu_ref[...])
    a = (_silu(g) * u).astype(_BF16)
    o_ref[...] += 0.5 * _dot(a, wd_ref[...])


def _ffn(h, norm_w, w_gu, w_down, *, tm, tf):
    m, d = h.shape
    f = w_down.shape[0]
    nf = f // tf
    return pl.pallas_call(
        _ffn_kernel,
        grid=(m // tm, nf),
        in_specs=[
            pl.BlockSpec((tm, d), lambda i, j: (i, 0)),
            pl.BlockSpec((1, d), lambda i, j: (0, 0)),
            pl.BlockSpec((d, tf), lambda i, j: (0, j)),
            pl.BlockSpec((d, tf), lambda i, j: (0, nf + j)),
            pl.BlockSpec((tf, d), lambda i, j: (j, 0)),
        ],
        out_specs=pl.BlockSpec((tm, d), lambda i, j: (i, 0)),
        out_shape=jax.ShapeDtypeStruct((m, d), _F32),
        scratch_shapes=[pltpu.VMEM((tm, d), _BF16)],
        compiler_params=_params("parallel", "arbitrary"),
        name="ffn",
    )(h, norm_w, w_gu, w_gu, w_down)


def _inproj_kernel(x_ref, nw_ref, w_ref, wg_ref, p_ref, gate_ref, n_ref):
    @pl.when(pl.program_id(1) == 0)
    def _():
        n = _rms(x_ref[...], nw_ref[...])
        n_hi = n.astype(_BF16)
        n_lo = (n - n_hi.astype(_F32)).astype(_BF16)
        n_ref[...] = n_hi
        s = _dot(n_hi, wg_ref[...]) + _dot(n_lo, wg_ref[...])
        lane = lax.broadcasted_iota(jnp.int32, s.shape, 1)
        folded = s + pltpu.roll(s, GATE_LANES - 2 * DN_HEADS, axis=1)
        gate_ref[...] = jnp.where(lane < 2 * DN_HEADS, folded, 0.0)

    res = _dot(n_ref[...], w_ref[...]).astype(_BF16)
    for c in range(p_ref.shape[0]):
        p_ref[c] = res[:, c * HEAD_DIM:(c + 1) * HEAD_DIM]


def _inproj(h, norm_w, w_main, w_gate, *, tm, tn):
    m, d = h.shape
    n_main = w_main.shape[1]
    hpt = tn // HEAD_DIM
    return pl.pallas_call(
        _inproj_kernel,
        grid=(m // tm, n_main // tn),
        in_specs=[
            pl.BlockSpec((tm, d), lambda i, j: (i, 0)),
            pl.BlockSpec((1, d), lambda i, j: (0, 0)),
            pl.BlockSpec((d, tn), lambda i, j: (0, j)),
            pl.BlockSpec((d, GATE_LANES), lambda i, j: (0, 0)),
        ],
        out_specs=[
            pl.BlockSpec((hpt, tm, HEAD_DIM), lambda i, j: (j, i, 0)),
            pl.BlockSpec((tm, GATE_LANES), lambda i, j: (i, 0)),
        ],
        out_shape=[
            jax.ShapeDtypeStruct((n_main // HEAD_DIM, m, HEAD_DIM), _BF16),
            jax.ShapeDtypeStruct((m, GATE_LANES), _F32),
        ],
        scratch_shapes=[pltpu.VMEM((tm, d), _BF16)],
        compiler_params=_params("parallel", "arbitrary"),
        name="in_proj",
    )(h, norm_w, w_main, w_gate)


def _dn_kernel(q_ref, k_ref, v_ref, z_ref, gate_ref, cq_ref, ck_ref, cv_ref, gpar_ref, onorm_ref,
               o_ref,
               xp_ref, qn_ref, kn_ref, vc_ref, gcb_ref, bb_ref, cs_ref, gall_ref, ball_ref,
               u_ref, wq_ref, a_ref, kd_ref, egl_ref):
    nh, t, hd = q_ref.shape
    c = DN_CHUNK
    gr = DN_GROUP_ROWS
    head0 = pl.program_id(1) * nh

    gates = gate_ref[...]
    gpar = gpar_ref[...]
    ball_ref[...] = jax.nn.sigmoid(gates)
    x = -jnp.exp(gpar[0:1, :]) * _softplus(gates + gpar[1:2, :])
    pos = lax.broadcasted_iota(jnp.int32, (t, GATE_LANES), 0) & (c - 1)
    cs_ref[0:CUMSUM_PAD, :] = jnp.zeros((CUMSUM_PAD, GATE_LANES), _F32)
    s = 1
    while s < c:
        cs_ref[CUMSUM_PAD:CUMSUM_PAD + t, :] = x
        x = x + jnp.where(pos >= s, cs_ref[CUMSUM_PAD - s:CUMSUM_PAD - s + t, :], 0.0)
        s *= 2
    gall_ref[...] = x
    xp_ref[0:CONV_PAD, :] = jnp.zeros((CONV_PAD, hd), _F32)

    row = lax.broadcasted_iota(jnp.int32, (gr, gr), 0)
    col = lax.broadcasted_iota(jnp.int32, (gr, gr), 1)
    same = (row // c) == (col // c)
    eye = row == col
    lower_incl = same & (row >= col)
    strict = same & (row > col)
    eye_c = jnp.where(lax.broadcasted_iota(jnp.int32, (c, gr), 0)
                      == (lax.broadcasted_iota(jnp.int32, (c, gr), 1) & (c - 1)), 1.0, 0.0)

    def conv_silu(x_ref, w):
        xp_ref[CONV_PAD:CONV_PAD + t, :] = x_ref[...].astype(_F32)
        acc = w[CONV_K - 1:CONV_K, :] * xp_ref[CONV_PAD:CONV_PAD + t, :]
        for sh in range(1, CONV_K):
            acc = acc + w[CONV_K - 1 - sh:CONV_K - sh, :] * xp_ref[CONV_PAD - sh:CONV_PAD - sh + t, :]
        return _silu(acc)

    def l2n(y):
        return y * lax.rsqrt(jnp.sum(y * y, axis=-1, keepdims=True) + L2_EPS)

    def solve_groups(rows):
        n = range(len(rows))
        q = [qn_ref[pl.ds(r, gr), :] for r in rows]
        k = [kn_ref[pl.ds(r, gr), :] for r in rows]
        v = [vc_ref[pl.ds(r, gr), :] for r in rows]
        bb = [bb_ref[pl.ds(r, gr), :] for r in rows]
        gcb = [gcb_ref[pl.ds(r, gr), :] for r in rows]
        k16 = [k[i].astype(_BF16) for i in n]
        kb = [k[i] * bb[i] for i in n]
        kk = [_dot_nt(kb[i].astype(_BF16), k16[i]) for i in n]
        qk = [_dot_nt(q[i].astype(_BF16), k16[i]) for i in n]
        eg = [jnp.exp(gcb[i]) for i in n]
        lmat, attn, sol = [], [], []
        for i in n:
            gc_sq = jnp.concatenate([gcb[i]] * (gr // hd), axis=-1)
            gc_row = jnp.sum(jnp.where(eye, gc_sq, 0.0), axis=0, keepdims=True)
            decay = jnp.exp(jnp.where(lower_incl, gc_sq - gc_row, -jnp.inf))
            lmat.append(jnp.where(strict, kk[i] * decay, 0.0))
            attn.append((qk[i] * decay).astype(_BF16))
            sol.append(jnp.concatenate([v[i] * bb[i], kb[i] * eg[i]], axis=-1).astype(_BF16))

        def compact(mat):
            out = mat[0:c]
            for j in range(1, DN_GROUP):
                out = out + mat[j * c:(j + 1) * c]
            return out

        def blockdiag(mat):
            return jnp.where(same, jnp.concatenate([mat] * DN_GROUP, axis=0), 0.0).astype(_BF16)

        pw = [compact(lmat[i]) for i in n]
        tinv = [eye_c - pw[i] for i in n]
        bd = [lmat[i].astype(_BF16) for i in n]
        m = 2
        while m < c:
            pw = [_dot(pw[i].astype(_BF16), bd[i]) for i in n]
            bd = [blockdiag(pw[i]) for i in n]
            tinv = [tinv[i] + _dot(tinv[i].astype(_BF16), bd[i]) for i in n]
            m *= 2
        sol = [_dot(blockdiag(tinv[i]), sol[i]) for i in n]

        res = []
        for i in n:
            g_last = jnp.concatenate(
                [jnp.broadcast_to(gcb[i][(j + 1) * c - 1:(j + 1) * c, :], (c, hd)) for j in range(DN_GROUP)], axis=0)
            w16 = sol[i][:, hd:].astype(_BF16)
            qg16 = (q[i] * eg[i]).astype(_BF16)
            kd16 = (k[i] * jnp.exp(g_last - gcb[i])).astype(_BF16)
            res.append((sol[i][:, :hd], w16, qg16, kd16, attn[i], eg[i]))
        return res

    def store_group(hl, r, u, w16, qg16, kd16, attn16, eg):
        u_ref[hl, pl.ds(r, gr), :] = u
        kd_ref[hl, pl.ds(r, gr), :] = kd16
        for i in range(DN_GROUP):
            lo, hi = i * c, (i + 1) * c
            r2 = pl.multiple_of(2 * r + 2 * lo, c)
            wq_ref[hl, pl.ds(r2, c), :] = w16[lo:hi]
            wq_ref[hl, pl.ds(r2 + c, c), :] = qg16[lo:hi]
            a_ref[hl, pl.ds(pl.multiple_of(r + lo, c), c), :] = attn16[lo:hi, lo:hi]
            r8 = pl.multiple_of((r + lo) // (c // 8), 8)
            egl_ref[hl, pl.ds(r8, 8), :] = jnp.broadcast_to(eg[hi - 1:hi, :], (8, hd))

    def head_prologue(hl, carry):
        lane = lax.broadcasted_iota(jnp.int32, (t, GATE_LANES), 1)
        head = head0 + hl
        g_col = jnp.sum(jnp.where(lane == head, gall_ref[...], 0.0), axis=-1, keepdims=True)
        b_col = jnp.sum(jnp.where(lane == head + DN_HEADS, ball_ref[...], 0.0), axis=-1, keepdims=True)
        gcb_ref[...] = jnp.broadcast_to(g_col, (t, hd))
        bb_ref[...] = jnp.broadcast_to(b_col, (t, hd))
        qn_ref[...] = l2n(conv_silu(q_ref.at[hl], cq_ref[hl])) * (hd ** -0.5)
        kn_ref[...] = l2n(conv_silu(k_ref.at[hl], ck_ref[hl]))
        vc_ref[...] = conv_silu(v_ref.at[hl], cv_ref[hl])

        def groups(gi, cc):
            rows = [pl.multiple_of((gi * DN_GROUPS_PER_ITER + sub) * gr, gr) for sub in range(DN_GROUPS_PER_ITER)]
            solved = solve_groups(rows)
            for r, res in zip(rows, solved):
                store_group(hl, r, *res)
            return cc

        lax.fori_loop(0, t // (gr * DN_GROUPS_PER_ITER), groups, 0)
        return carry

    lax.fori_loop(0, nh, head_prologue, 0)

    onorm = onorm_ref[...]

    def chunk(ci, states):
        r = pl.multiple_of(ci * c, c)
        r2 = pl.multiple_of(ci * 2 * c, 2 * c)
        r8 = pl.multiple_of(ci * 8, 8)
        heads = range(nh)
        wq = [wq_ref[hl, pl.ds(r2, 2 * c), :] for hl in heads]
        u = [u_ref[hl, pl.ds(r, c), :] for hl in heads]
        attn = [a_ref[hl, pl.ds(r, c), :] for hl in heads]
        kd = [kd_ref[hl, pl.ds(r, c), :] for hl in heads]
        e_last = [egl_ref[hl, pl.ds(r8, 8), :][0:1] for hl in heads]
        z = [z_ref[hl, pl.ds(r, c), :] for hl in heads]
        ws = [_dot(wq[hl], states[hl].astype(_BF16)) for hl in heads]
        v_new = [(u[hl] - ws[hl][:c]).astype(_BF16) for hl in heads]
        new_states = tuple(states[hl] * e_last[hl] + _dot_tn(kd[hl], v_new[hl]) for hl in heads)
        outs = []
        for hl in heads:
            o = ws[hl][c:] + _dot(attn[hl], v_new[hl])
            y = o * lax.rsqrt(jnp.mean(o * o, axis=-1, keepdims=True) + RMS_EPS) * onorm
            outs.append((y * _silu(z[hl].astype(_F32))).astype(_BF16))
        for hl in heads:
            o_ref[pl.ds(r, c), hl * hd:(hl + 1) * hd] = outs[hl]
        return new_states

    lax.fori_loop(0, t // c, chunk, tuple(jnp.zeros((hd, hd), _F32) for _ in range(nh)))


def _deltanet(p_heads, gates, conv_w, gate_par, out_norm, *, batch, seq):
    hd = HEAD_DIM
    nh = DN_HEADS_PER_STEP
    ng = DN_HEADS // nh
    c = DN_CHUNK
    assert seq % (DN_GROUP_ROWS * DN_GROUPS_PER_ITER) == 0, seq
    heads = lambda part: pl.BlockSpec((nh, seq, hd), lambda b, g: (part * ng + g, b, 0))
    convw = lambda part: pl.BlockSpec((nh, CONV_K, hd), lambda b, g: (part * ng + g, 0, 0))
    f32_rows = lambda: pltpu.VMEM((seq, hd), _F32)
    return pl.pallas_call(
        _dn_kernel,
        grid=(batch, ng),
        in_specs=[
            heads(0), heads(1), heads(2), heads(3),
            pl.BlockSpec((seq, GATE_LANES), lambda b, g: (b, 0)),
            convw(0), convw(1), convw(2),
            pl.BlockSpec((2, GATE_LANES), lambda b, g: (0, 0)),
            pl.BlockSpec((1, hd), lambda b, g: (0, 0)),
        ],
        out_specs=pl.BlockSpec((seq, nh * hd), lambda b, g: (b, g)),
        out_shape=jax.ShapeDtypeStruct((batch * seq, DN_HEADS * hd), _BF16),
        scratch_shapes=[
            pltpu.VMEM((CONV_PAD + seq, hd), _F32),
            f32_rows(), f32_rows(), f32_rows(),
            f32_rows(), f32_rows(),
            pltpu.VMEM((CUMSUM_PAD + seq, GATE_LANES), _F32),
            f32_rows(), f32_rows(),
            pltpu.VMEM((nh, seq, hd), _F32),
            pltpu.VMEM((nh, 2 * seq, hd), _BF16),
            pltpu.VMEM((nh, seq, c), _BF16),
            pltpu.VMEM((nh, seq, hd), _BF16),
            pltpu.VMEM((nh, seq // c * 8, hd), _F32),
        ],
        compiler_params=_params("parallel", "arbitrary"),
        name="deltanet",
    )(p_heads, p_heads, p_heads, p_heads, gates, conv_w, conv_w, conv_w, gate_par, out_norm)


def _sb_kernel(q_ref, k_ref, v_ref, o_ref):
    nh, tq, hd = q_ref.shape
    tk = SB_TK
    qi = pl.program_id(2)
    row = lax.broadcasted_iota(jnp.int32, (tq, tk), 0)
    col = lax.broadcasted_iota(jnp.int32, (tq, tk), 1)
    causal = col < row
    ntri = jnp.where(row >= col, -1.0, 0.0).astype(_BF16)

    def tiles(kt, nt, state, masked):
        heads = range(nh)
        steps = range(nt)
        ks = [pl.multiple_of((kt - j) * tk, tk) for j in steps]
        k_ = [[k_ref[h, pl.ds(ks[j], tk), :] for h in heads] for j in steps]
        v_ = [[v_ref[h, pl.ds(ks[j], tk), :] for h in heads] for j in steps]
        z2 = [[_dot_nt(q_ref[h], k_[j][h]) for h in heads] for j in steps]
        revs = []
        for j in steps:
            revs.append([])
            for h in heads:
                neg_abs = pltpu.bitcast(pltpu.bitcast(z2[j][h], jnp.uint32) | jnp.uint32(0x80000000), _F32)
                x = jnp.maximum(z2[j][h], 0.0) + jnp.log2(1.0 + jnp.exp2(neg_abs))
                if masked:
                    x = jnp.where(causal, x, 0.0)
                revs[j].append(_dot(x.astype(_BF16), ntri))
        out = []
        for h in heads:
            acc, carry = state[2 * h], state[2 * h + 1]
            for j in steps:
                a = jnp.exp2(z2[j][h] + revs[j][h] + jnp.concatenate([carry] * (tk // hd), axis=-1))
                if masked:
                    a = jnp.where(causal, a, 0.0)
                acc = acc + _dot(a.astype(_BF16), v_[j][h])
                carry = carry + jnp.broadcast_to(revs[j][h][:, 0:1], (tq, hd))
            out += [acc, carry]
        return tuple(out)

    zeros = jnp.zeros((tq, hd), _F32)
    state = tiles(qi, 1, (zeros,) * (2 * nh), True)
    odd = qi % 2
    state = lax.fori_loop(0, odd, lambda i, st: tiles(qi - 1, 1, st, False), state)
    state = lax.fori_loop(0, qi // 2, lambda i, st: tiles(qi - 1 - odd - 2 * i, 2, st, False), state)
    for h in range(nh):
        o_ref[:, h * hd:(h + 1) * hd] = state[2 * h].astype(_BF16)


def _stickbreak(p_heads, *, batch, seq, head_off):
    hd = HEAD_DIM
    nh = SB_HEADS_PER_STEP
    ng = SB_HEADS // nh
    nq = seq // SB_TQ
    assert SB_TQ == SB_TK and seq % SB_TQ == 0, seq
    base = head_off // nh
    return pl.pallas_call(
        _sb_kernel,
        grid=(batch, ng, nq),
        in_specs=[
            pl.BlockSpec((nh, SB_TQ, hd), lambda b, g, i: (base + g, b * nq + i, 0)),
            pl.BlockSpec((nh, seq, hd), lambda b, g, i: (base + ng + g, b, 0)),
            pl.BlockSpec((nh, seq, hd), lambda b, g, i: (base + 2 * ng + g, b, 0)),
        ],
        out_specs=pl.BlockSpec((SB_TQ, nh * hd), lambda b, g, i: (b * nq + i, g)),
        out_shape=jax.ShapeDtypeStruct((batch * seq, SB_HEADS * hd), _BF16),
        compiler_params=_params("parallel", "parallel", "arbitrary"),
        name="stickbreak",
    )(p_heads, p_heads, p_heads)


def _outproj_kernel(h_ref, a_ref, b_ref, wa_ref, wb_ref, o_ref):
    o_ref[...] = h_ref[...] + _dot(a_ref[...], wa_ref[...]) + _dot(b_ref[...], wb_ref[...])


def _outproj(h, o_dn, o_sb, w_out, *, tm):
    m, d = h.shape
    ka = o_dn.shape[1]
    kb = o_sb.shape[1]
    assert ka == kb
    return pl.pallas_call(
        _outproj_kernel,
        grid=(m // tm,),
        in_specs=[
            pl.BlockSpec((tm, d), lambda i: (i, 0)),
            pl.BlockSpec((tm, ka), lambda i: (i, 0)),
            pl.BlockSpec((tm, kb), lambda i: (i, 0)),
            pl.BlockSpec((ka, d), lambda i: (0, 0)),
            pl.BlockSpec((kb, d), lambda i: (1, 0)),
        ],
        out_specs=pl.BlockSpec((tm, d), lambda i: (i, 0)),
        out_shape=jax.ShapeDtypeStruct((m, d), _F32),
        compiler_params=_params("parallel"),
        name="out_proj",
    )(h, o_dn, o_sb, w_out, w_out)


def _ple_kernel(h_ref, p_ref, nw_ref, wg_ref, wp_ref, fw_ref, o_ref):
    h = h_ref[...]
    n = _rms(h, nw_ref[...]).astype(_BF16)
    gate = jax.nn.sigmoid(_dot(n, wg_ref[...]))
    h = h + gate * _dot(p_ref[...].astype(_BF16), wp_ref[...])
    o_ref[...] = _rms(h, fw_ref[...])


def _ple(h, p, norm_w, w_gate, w_proj, final_w, *, tm):
    m, d = h.shape
    pd = p.shape[1]
    return pl.pallas_call(
        _ple_kernel,
        grid=(m // tm,),
        in_specs=[
            pl.BlockSpec((tm, d), lambda i: (i, 0)),
            pl.BlockSpec((tm, pd), lambda i: (i, 0)),
            pl.BlockSpec((1, d), lambda i: (0, 0)),
            pl.BlockSpec((d, d), lambda i: (0, 0)),
            pl.BlockSpec((pd, d), lambda i: (0, 0)),
            pl.BlockSpec((1, d), lambda i: (0, 0)),
        ],
        out_specs=pl.BlockSpec((tm, d), lambda i: (i, 0)),
        out_shape=jax.ShapeDtypeStruct((m, d), _F32),
        compiler_params=_params("parallel"),
        name="ple_final",
    )(h, p, norm_w, w_gate, w_proj, final_w)


def _pad_lanes(v, width):
    return jnp.pad(v.astype(_F32), (0, width - v.shape[0]))


def kernel(x, p, ffn1_norm, ffn1_w_gu, ffn1_w_down, mix_norm, w_in, dn_conv, dn_a_log, dn_dt_bias, dn_out_norm, w_out, ffn2_norm, ffn2_w_gu, ffn2_w_down, ple_norm, ple_w_gate, ple_w_proj, final_norm):
    batch, seq, d = x.shape
    depth = p.shape[0]
    m = batch * seq
    dn_w = DN_HEADS * HEAD_DIM
    gate_lo = 4 * dn_w
    gate_hi = gate_lo + 2 * DN_HEADS
    row = lambda v: v.reshape(1, -1).astype(_F32)

    h = x.reshape(m, d)
    for i in range(depth):
        h = _ffn(h, row(ffn1_norm[i]), ffn1_w_gu[i].astype(_BF16), ffn1_w_down[i].astype(_BF16),
                 tm=1024, tf=512)

        wi = w_in[i]
        sb_q = gate_hi + SB_HEADS * HEAD_DIM
        w_main = jnp.concatenate([wi[:, :gate_lo].astype(_BF16),
                                  (wi[:, gate_hi:sb_q] * ((HEAD_DIM ** -0.5) * LOG2E)).astype(_BF16),
                                  wi[:, sb_q:].astype(_BF16)], axis=1)
        w_gate = wi[:, gate_lo:gate_hi]
        w_gate_hi = w_gate.astype(_BF16)
        w_gate_lo = (w_gate - w_gate_hi.astype(_F32)).astype(_BF16)
        w_gate = jnp.pad(jnp.concatenate([w_gate_hi, w_gate_lo], axis=1), ((0, 0), (0, GATE_LANES - 4 * DN_HEADS)))
        p_heads, gates = _inproj(h, row(mix_norm[i]), w_main, w_gate, tm=1024, tn=1024)

        gate_par = jnp.stack([_pad_lanes(dn_a_log[i], GATE_LANES), _pad_lanes(dn_dt_bias[i], GATE_LANES)])
        conv_w = dn_conv[i].astype(_F32).reshape(CONV_K, 3 * DN_HEADS, HEAD_DIM).transpose(1, 0, 2)
        o_dn = _deltanet(p_heads, gates, conv_w, gate_par, row(dn_out_norm[i]), batch=batch, seq=seq)
        o_sb = _stickbreak(p_heads, batch=batch, seq=seq, head_off=4 * DN_HEADS)
        h = _outproj(h, o_dn, o_sb, w_out[i].astype(_BF16), tm=512)

        h = _ffn(h, row(ffn2_norm[i]), ffn2_w_gu[i].astype(_BF16), ffn2_w_down[i].astype(_BF16),
                 tm=1024, tf=512)
        assert i == depth - 1, "the final RMSNorm is fused into the last layer's embedding kernel"
        h = _ple(h, p[i].reshape(m, -1), row(ple_norm[i]), ple_w_gate[i].astype(_BF16),
                 ple_w_proj[i].astype(_BF16), row(final_norm), tm=512)
    return h.reshape(batch, seq, d)
```

```python
import functools
import math

import jax
import jax.numpy as jnp
from jax import lax
from jax.experimental import pallas as pl
from jax.experimental.pallas import tpu as pltpu

_F32 = jnp.float32
_BF16 = jnp.bfloat16

RMS_EPS = 1e-6
L2_EPS = 1e-6
DN_HEADS = 8
SB_HEADS = 8
HEAD_DIM = 128
CONV_K = 4
GATE_LANES = 128
CONV_PAD = 8

DN_CHUNK = 64
DN_GROUP = 4
DN_GROUP_ROWS = DN_CHUNK * DN_GROUP
DN_GROUPS_PER_ITER = 8
DN_HEADS_PER_STEP = 4
CUMSUM_PAD = DN_CHUNK // 2
DN_CHUNK_ROWS = HEAD_DIM + DN_CHUNK

SB_TQ = 256
SB_TK = 256
SB_HEADS_PER_STEP = 8

VMEM_LIMIT = 56 * 1024 * 1024
LOG2E = math.log2(math.e)


def _dot(a, b):
    return jnp.dot(a, b, preferred_element_type=_F32)


def _dot_nt(a, b):
    return lax.dot_general(a, b, (((1,), (1,)), ((), ())), preferred_element_type=_F32)


def _dot_tn(a, b):
    return lax.dot_general(a, b, (((0,), (0,)), ((), ())), preferred_element_type=_F32)


def _rms(x, w):
    return x * lax.rsqrt(jnp.mean(x * x, axis=-1, keepdims=True) + RMS_EPS) * w


def _softplus(x):
    return jnp.maximum(x, 0.0) + jnp.log1p(jnp.exp(-jnp.abs(x)))


def _silu(x):
    return x * jax.nn.sigmoid(x)


def _params(*sem):
    return pltpu.CompilerParams(dimension_semantics=sem, vmem_limit_bytes=VMEM_LIMIT)


def _ffn_kernel(x_ref, nw_ref, wg_ref, wu_ref, wd_ref, o_ref, n_ref):
    @pl.when(pl.program_id(1) == 0)
    def _():
        x = x_ref[...]
        n_ref[...] = _rms(x, nw_ref[...]).astype(_BF16)
        o_ref[...] = x

    n = n_ref[...]
    g = _dot(n, wg_ref[...])
    u = _dot(n, wu_ref[...])
    a = (_silu(g) * u).astype(_BF16)
    o_ref[...] += 0.5 * _dot(a, wd_ref[...])


def _ffn(h, norm_w, w_gu, w_down, *, tm, tf):
    m, d = h.shape
    f = w_down.shape[0]
    nf = f // tf
    return pl.pallas_call(
        _ffn_kernel,
        grid=(m // tm, nf),
        in_specs=[
            pl.BlockSpec((tm, d), lambda i, j: (i, 0)),
            pl.BlockSpec((1, d), lambda i, j: (0, 0)),
            pl.BlockSpec((d, tf), lambda i, j: (0, j)),
            pl.BlockSpec((d, tf), lambda i, j: (0, nf + j)),
            pl.BlockSpec((tf, d), lambda i, j: (j, 0)),
        ],
        out_specs=pl.BlockSpec((tm, d), lambda i, j: (i, 0)),
        out_shape=jax.ShapeDtypeStruct((m, d), _F32),
        scratch_shapes=[pltpu.VMEM((tm, d), _BF16)],
        compiler_params=_params("parallel", "arbitrary"),
        name="ffn",
    )(h, norm_w, w_gu, w_gu, w_down)


def _inproj_kernel(na, x_ref, nw_ref, wa_ref, wb_ref, wg_ref, p_ref, gate_ref, n_ref):
    @pl.when(pl.program_id(1) == 0)
    def _():
        n = _rms(x_ref[...], nw_ref[...])
        n_hi = n.astype(_BF16)
        n_lo = (n - n_hi.astype(_F32)).astype(_BF16)
        n_ref[...] = n_hi
        s = _dot(n_hi, wg_ref[...]) + _dot(n_lo, wg_ref[...])
        lane = lax.broadcasted_iota(jnp.int32, s.shape, 1)
        folded = s + pltpu.roll(s, GATE_LANES - 2 * DN_HEADS, axis=1)
        gate_ref[...] = jnp.where(lane < 2 * DN_HEADS, folded, 0.0)

    def project(w_ref):
        res = _dot(n_ref[...], w_ref[...]).astype(_BF16)
        for c in range(p_ref.shape[0]):
            p_ref[c] = res[:, c * HEAD_DIM:(c + 1) * HEAD_DIM]

    @pl.when(pl.program_id(1) < na)
    def _():
        project(wa_ref)

    @pl.when(pl.program_id(1) >= na)
    def _():
        project(wb_ref)


def _inproj(h, norm_w, w_a, w_b, w_gate, *, tm, tn):
    m, d = h.shape
    na, nb = w_a.shape[1] // tn, w_b.shape[1] // tn
    n_main = w_a.shape[1] + w_b.shape[1]
    hpt = tn // HEAD_DIM
    return pl.pallas_call(
        functools.partial(_inproj_kernel, na),
        grid=(m // tm, na + nb),
        in_specs=[
            pl.BlockSpec((tm, d), lambda i, j: (i, 0)),
            pl.BlockSpec((1, d), lambda i, j: (0, 0)),
            pl.BlockSpec((d, tn), lambda i, j: (0, jnp.minimum(j, na - 1))),
            pl.BlockSpec((d, tn), lambda i, j: (0, jnp.maximum(j - na, 0))),
            pl.BlockSpec((d, GATE_LANES), lambda i, j: (0, 0)),
        ],
        out_specs=[
            pl.BlockSpec((hpt, tm, HEAD_DIM), lambda i, j: (j, i, 0)),
            pl.BlockSpec((tm, GATE_LANES), lambda i, j: (i, 0)),
        ],
        out_shape=[
            jax.ShapeDtypeStruct((n_main // HEAD_DIM, m, HEAD_DIM), _BF16),
            jax.ShapeDtypeStruct((m, GATE_LANES), _F32),
        ],
        scratch_shapes=[pltpu.VMEM((tm, d), _BF16)],
        compiler_params=_params("parallel", "arbitrary"),
        name="in_proj",
    )(h, norm_w, w_a, w_b, w_gate)


def _dn_kernel(q_ref, k_ref, v_ref, z_ref, gate_ref, cq_ref, ck_ref, cv_ref, gpar_ref, onorm_ref,
               o_ref,
               xp_ref, qn_ref, kn_ref, vc_ref, gcb_ref, bb_ref, cs_ref, gsel_ref,
               kq_ref, no_ref, egl_ref):
    nh, t, hd = q_ref.shape
    c = DN_CHUNK
    gr = DN_GROUP_ROWS
    cr = DN_CHUNK_ROWS
    head0 = pl.program_id(1) * nh

    gates = gate_ref[...]
    gpar = gpar_ref[...]
    x = -jnp.exp(gpar[0:1, :]) * _softplus(gates + gpar[1:2, :])
    pos = lax.broadcasted_iota(jnp.int32, (t, GATE_LANES), 0) & (c - 1)
    cs_ref[0:CUMSUM_PAD, :] = jnp.zeros((CUMSUM_PAD, GATE_LANES), _F32)
    s = 1
    while s < c:
        cs_ref[CUMSUM_PAD:CUMSUM_PAD + t, :] = x
        x = x + jnp.where(pos >= s, cs_ref[CUMSUM_PAD - s:CUMSUM_PAD - s + t, :], 0.0)
        s *= 2
    lane = lax.broadcasted_iota(jnp.int32, (t, GATE_LANES), 1)
    comb = jnp.where(lane < DN_HEADS, x, jax.nn.sigmoid(gates))
    comb_hi = comb.astype(_BF16)
    gsel_ref[:, :GATE_LANES] = comb_hi
    gsel_ref[:, GATE_LANES:] = (comb - comb_hi.astype(_F32)).astype(_BF16)
    xp_ref[0:CONV_PAD, :] = jnp.zeros((CONV_PAD, hd), _F32)

    row = lax.broadcasted_iota(jnp.int32, (gr, gr), 0)
    col = lax.broadcasted_iota(jnp.int32, (gr, gr), 1)
    same = (row // c) == (col // c)
    eye = row == col
    lower_incl = same & (row >= col)
    strict = same & (row > col)
    eye_c = jnp.where(lax.broadcasted_iota(jnp.int32, (c, gr), 0)
                      == (lax.broadcasted_iota(jnp.int32, (c, gr), 1) & (c - 1)), 1.0, 0.0)

    def conv_silu(x_ref, w):
        xp_ref[CONV_PAD:CONV_PAD + t, :] = x_ref[...].astype(_F32)
        acc = w[CONV_K - 1:CONV_K, :] * xp_ref[CONV_PAD:CONV_PAD + t, :]
        for sh in range(1, CONV_K):
            acc = acc + w[CONV_K - 1 - sh:CONV_K - sh, :] * xp_ref[CONV_PAD - sh:CONV_PAD - sh + t, :]
        return _silu(acc)

    def l2n(y, scale):
        return y * (lax.rsqrt(jnp.sum(y * y, axis=-1, keepdims=True) + L2_EPS) * scale)

    def solve_groups(rows):
        n = range(len(rows))
        q = [qn_ref[pl.ds(r, gr), :] for r in rows]
        k = [kn_ref[pl.ds(r, gr), :] for r in rows]
        v = [vc_ref[pl.ds(r, gr), :] for r in rows]
        bb = [bb_ref[pl.ds(r, gr), :] for r in rows]
        gcb = [gcb_ref[pl.ds(r, gr), :] for r in rows]
        k16 = [k[i].astype(_BF16) for i in n]
        kb = [k[i] * bb[i] for i in n]
        kk = [_dot_nt(kb[i].astype(_BF16), k16[i]) for i in n]
        qk = [_dot_nt(q[i].astype(_BF16), k16[i]) for i in n]
        eg = [jnp.exp(gcb[i]) for i in n]
        lmat, attn, sol = [], [], []
        for i in n:
            gc_sq = jnp.concatenate([gcb[i]] * (gr // hd), axis=-1)
            gc_row = jnp.sum(jnp.where(eye, gc_sq, 0.0), axis=0, keepdims=True)
            decay = jnp.exp(jnp.where(lower_incl, gc_sq - gc_row, -jnp.inf))
            lmat.append(jnp.where(strict, kk[i] * decay, 0.0))
            attn.append((qk[i] * decay).astype(_BF16))
            sol.append(jnp.concatenate([v[i] * bb[i], kb[i] * eg[i]], axis=-1).astype(_BF16))

        def compact(mat):
            out = mat[0:c]
            for j in range(1, DN_GROUP):
                out = out + mat[j * c:(j + 1) * c]
            return out

        def blockdiag(mat):
            return jnp.where(same, jnp.concatenate([mat] * DN_GROUP, axis=0), 0.0).astype(_BF16)

        pw = [compact(lmat[i]) for i in n]
        tinv = [eye_c - pw[i] for i in n]
        bd = [lmat[i].astype(_BF16) for i in n]
        m = 2
        while m < c:
            pw = [_dot(pw[i].astype(_BF16), bd[i]) for i in n]
            bd = [blockdiag(pw[i]) for i in n]
            tinv = [tinv[i] + _dot(tinv[i].astype(_BF16), bd[i]) for i in n]
            m *= 2
        sol = [_dot(blockdiag(tinv[i]), sol[i]).astype(_BF16) for i in n]
        auw = [_dot(attn[i], sol[i]) for i in n]
        kd = []
        for i in n:
            g_last = jnp.concatenate(
                [jnp.broadcast_to(gcb[i][(j + 1) * c - 1:(j + 1) * c, :], (c, hd)) for j in range(DN_GROUP)], axis=0)
            kd.append((k[i] * jnp.exp(g_last - gcb[i])).astype(_BF16))
        kuw = [[_dot_tn(kd[i][j * c:(j + 1) * c], sol[i][j * c:(j + 1) * c]) for j in range(DN_GROUP)] for i in n]
        res = []
        for i in n:
            qp16 = (q[i] * eg[i] - auw[i][:, hd:]).astype(_BF16)
            o0_16 = auw[i][:, :hd].astype(_BF16)
            res.append((qp16, o0_16, [m_.astype(_BF16) for m_ in kuw[i]], eg[i]))
        return res

    def store_group(hl, r, qp16, o0_16, kuw16, eg):
        for j in range(DN_GROUP):
            lo, hi = j * c, (j + 1) * c
            base = pl.multiple_of((r + lo) // c * cr, cr)
            kq_ref[hl, pl.ds(base, hd), :] = kuw16[j][:, hd:]
            kq_ref[hl, pl.ds(base + hd, c), :] = qp16[lo:hi]
            no_ref[hl, pl.ds(base, hd), :] = kuw16[j][:, :hd]
            no_ref[hl, pl.ds(base + hd, c), :] = o0_16[lo:hi]
            r8 = pl.multiple_of((r + lo) // (c // 8), 8)
            egl_ref[hl, pl.ds(r8, 8), :] = jnp.broadcast_to(eg[hi - 1:hi, :], (8, hd))

    def head_prologue(hl, carry):
        head = head0 + hl
        krow = lax.broadcasted_iota(jnp.int32, (2 * GATE_LANES, 2 * hd), 0) & (GATE_LANES - 1)
        ncol = lax.broadcasted_iota(jnp.int32, (2 * GATE_LANES, 2 * hd), 1)
        sel = jnp.where(krow == jnp.where(ncol < hd, head, head + DN_HEADS), 1.0, 0.0).astype(_BF16)
        picked = _dot(gsel_ref[...], sel)
        gcb_ref[...] = picked[:, :hd]
        bb_ref[...] = picked[:, hd:]
        qn_ref[...] = l2n(conv_silu(q_ref.at[hl], cq_ref[hl]), hd ** -0.5)
        kn_ref[...] = l2n(conv_silu(k_ref.at[hl], ck_ref[hl]), 1.0)
        vc_ref[...] = conv_silu(v_ref.at[hl], cv_ref[hl])

        def groups(gi, cc):
            rows = [pl.multiple_of((gi * DN_GROUPS_PER_ITER + sub) * gr, gr) for sub in range(DN_GROUPS_PER_ITER)]
            solved = solve_groups(rows)
            for r, res in zip(rows, solved):
                store_group(hl, r, *res)
            return cc

        lax.fori_loop(0, t // (gr * DN_GROUPS_PER_ITER), groups, 0)
        return carry

    lax.fori_loop(0, nh, head_prologue, 0)

    onorm = onorm_ref[...]

    def chunk(ci, states):
        r = pl.multiple_of(ci * c, c)
        rc = pl.multiple_of(ci * cr, cr)
        r8 = pl.multiple_of(ci * 8, 8)
        heads = range(nh)
        kq = [kq_ref[hl, pl.ds(rc, cr), :] for hl in heads]
        no = [no_ref[hl, pl.ds(rc, cr), :] for hl in heads]
        e_last = [egl_ref[hl, pl.ds(r8, 8), :][0:1] for hl in heads]
        z = [z_ref[hl, pl.ds(r, c), :] for hl in heads]
        ks = [_dot(kq[hl], states[hl].astype(_BF16)) for hl in heads]
        new_states = tuple(states[hl] * e_last[hl] - ks[hl][:hd] + no[hl][:hd].astype(_F32) for hl in heads)
        outs = []
        for hl in heads:
            o = ks[hl][hd:] + no[hl][hd:].astype(_F32)
            y = o * lax.rsqrt(jnp.mean(o * o, axis=-1, keepdims=True) + RMS_EPS) * onorm
            outs.append((y * _silu(z[hl].astype(_F32))).astype(_BF16))
        for hl in heads:
            o_ref[pl.ds(r, c), hl * hd:(hl + 1) * hd] = outs[hl]
        return new_states

    lax.fori_loop(0, t // c, chunk, tuple(jnp.zeros((hd, hd), _F32) for _ in range(nh)))


def _deltanet(p_heads, gates, conv_w, gate_par, out_norm, *, batch, seq):
    hd = HEAD_DIM
    nh = DN_HEADS_PER_STEP
    ng = DN_HEADS // nh
    c = DN_CHUNK
    assert seq % (DN_GROUP_ROWS * DN_GROUPS_PER_ITER) == 0, seq
    heads = lambda part: pl.BlockSpec((nh, seq, hd), lambda b, g: (part * ng + g, b, 0))
    convw = lambda part: pl.BlockSpec((nh, CONV_K, hd), lambda b, g: (part * ng + g, 0, 0))
    f32_rows = lambda: pltpu.VMEM((seq, hd), _F32)
    return pl.pallas_call(
        _dn_kernel,
        grid=(batch, ng),
        in_specs=[
            heads(0), heads(1), heads(2), heads(3),
            pl.BlockSpec((seq, GATE_LANES), lambda b, g: (b, 0)),
            convw(0), convw(1), convw(2),
            pl.BlockSpec((2, GATE_LANES), lambda b, g: (0, 0)),
            pl.BlockSpec((1, hd), lambda b, g: (0, 0)),
        ],
        out_specs=pl.BlockSpec((seq, nh * hd), lambda b, g: (b, g)),
        out_shape=jax.ShapeDtypeStruct((batch * seq, DN_HEADS * hd), _BF16),
        scratch_shapes=[
            pltpu.VMEM((CONV_PAD + seq, hd), _F32),
            f32_rows(), f32_rows(), f32_rows(),
            f32_rows(), f32_rows(),
            pltpu.VMEM((CUMSUM_PAD + seq, GATE_LANES), _F32),
            pltpu.VMEM((seq, 2 * GATE_LANES), _BF16),
            pltpu.VMEM((nh, seq // c * DN_CHUNK_ROWS, hd), _BF16),
            pltpu.VMEM((nh, seq // c * DN_CHUNK_ROWS, hd), _BF16),
            pltpu.VMEM((nh, seq // c * 8, hd), _F32),
        ],
        compiler_params=_params("parallel", "arbitrary"),
        name="deltanet",
    )(p_heads, p_heads, p_heads, p_heads, gates, conv_w, conv_w, conv_w, gate_par, out_norm)


def _sb_kernel(q_ref, k_ref, v_ref, o_ref):
    nh, tq, hd = q_ref.shape
    tk = SB_TK
    qi = pl.program_id(2)
    row = lax.broadcasted_iota(jnp.int32, (tq, tk), 0)
    col = lax.broadcasted_iota(jnp.int32, (tq, tk), 1)
    causal = col < row
    ntri = jnp.where(row >= col, -1.0, 0.0).astype(_BF16)

    def tiles(kt, nt, state, masked):
        heads = range(nh)
        steps = range(nt)
        ks = [pl.multiple_of((kt - j) * tk, tk) for j in steps]
        k_ = [[k_ref[h, pl.ds(ks[j], tk), :] for h in heads] for j in steps]
        v_ = [[v_ref[h, pl.ds(ks[j], tk), :] for h in heads] for j in steps]
        z2 = [[_dot_nt(q_ref[h], k_[j][h]) for h in heads] for j in steps]
        revs = []
        for j in steps:
            revs.append([])
            for h in heads:
                neg_abs = pltpu.bitcast(pltpu.bitcast(z2[j][h], jnp.uint32) | jnp.uint32(0x80000000), _F32)
                x = jnp.maximum(z2[j][h], 0.0) + jnp.log2(1.0 + jnp.exp2(neg_abs))
                if masked:
                    x = jnp.where(causal, x, 0.0)
                revs[j].append(_dot(x.astype(_BF16), ntri))
        out = []
        for h in heads:
            acc, carry = state[2 * h], state[2 * h + 1]
            for j in steps:
                a = jnp.exp2(z2[j][h] + revs[j][h] + jnp.concatenate([carry] * (tk // hd), axis=-1))
                if masked:
                    a = jnp.where(causal, a, 0.0)
                acc = acc + _dot(a.astype(_BF16), v_[j][h])
                carry = carry + jnp.broadcast_to(revs[j][h][:, 0:1], (tq, hd))
            out += [acc, carry]
        return tuple(out)

    zeros = jnp.zeros((tq, hd), _F32)
    state = tiles(qi, 1, (zeros,) * (2 * nh), True)
    odd = qi % 2
    state = lax.fori_loop(0, odd, lambda i, st: tiles(qi - 1, 1, st, False), state)
    state = lax.fori_loop(0, qi // 2, lambda i, st: tiles(qi - 1 - odd - 2 * i, 2, st, False), state)
    for h in range(nh):
        o_ref[:, h * hd:(h + 1) * hd] = state[2 * h].astype(_BF16)


def _stickbreak(p_heads, *, batch, seq, head_off):
    hd = HEAD_DIM
    nh = SB_HEADS_PER_STEP
    ng = SB_HEADS // nh
    nq = seq // SB_TQ
    assert SB_TQ == SB_TK and seq % SB_TQ == 0, seq
    base = head_off // nh
    return pl.pallas_call(
        _sb_kernel,
        grid=(batch, ng, nq),
        in_specs=[
            pl.BlockSpec((nh, SB_TQ, hd), lambda b, g, i: (base + g, b * nq + i, 0)),
            pl.BlockSpec((nh, seq, hd), lambda b, g, i: (base + ng + g, b, 0)),
            pl.BlockSpec((nh, seq, hd), lambda b, g, i: (base + 2 * ng + g, b, 0)),
        ],
        out_specs=pl.BlockSpec((SB_TQ, nh * hd), lambda b, g, i: (b * nq + i, g)),
        out_shape=jax.ShapeDtypeStruct((batch * seq, SB_HEADS * hd), _BF16),
        compiler_params=_params("parallel", "parallel", "arbitrary"),
        name="stickbreak",
    )(p_heads, p_heads, p_heads)


def _outproj_kernel(h_ref, a_ref, b_ref, wa_ref, wb_ref, o_ref):
    o_ref[...] = h_ref[...] + _dot(a_ref[...], wa_ref[...]) + _dot(b_ref[...], wb_ref[...])


def _outproj(h, o_dn, o_sb, w_out, *, tm):
    m, d = h.shape
    ka = o_dn.shape[1]
    kb = o_sb.shape[1]
    assert ka == kb
    return pl.pallas_call(
        _outproj_kernel,
        grid=(m // tm,),
        in_specs=[
            pl.BlockSpec((tm, d), lambda i: (i, 0)),
            pl.BlockSpec((tm, ka), lambda i: (i, 0)),
            pl.BlockSpec((tm, kb), lambda i: (i, 0)),
            pl.BlockSpec((ka, d), lambda i: (0, 0)),
            pl.BlockSpec((kb, d), lambda i: (1, 0)),
        ],
        out_specs=pl.BlockSpec((tm, d), lambda i: (i, 0)),
        out_shape=jax.ShapeDtypeStruct((m, d), _F32),
        compiler_params=_params("parallel"),
        name="out_proj",
    )(h, o_dn, o_sb, w_out, w_out)


def _ple_kernel(h_ref, p_ref, nw_ref, wg_ref, wp_ref, fw_ref, o_ref):
    h = h_ref[...]
    n = _rms(h, nw_ref[...]).astype(_BF16)
    gate = jax.nn.sigmoid(_dot(n, wg_ref[...]))
    h = h + gate * _dot(p_ref[...].astype(_BF16), wp_ref[...])
    o_ref[...] = _rms(h, fw_ref[...])


def _ple(h, p, norm_w, w_gate, w_proj, final_w, *, tm):
    m, d = h.shape
    pd = p.shape[1]
    return pl.pallas_call(
        _ple_kernel,
        grid=(m // tm,),
        in_specs=[
            pl.BlockSpec((tm, d), lambda i: (i, 0)),
            pl.BlockSpec((tm, pd), lambda i: (i, 0)),
            pl.BlockSpec((1, d), lambda i: (0, 0)),
            pl.BlockSpec((d, d), lambda i: (0, 0)),
            pl.BlockSpec((pd, d), lambda i: (0, 0)),
            pl.BlockSpec((1, d), lambda i: (0, 0)),
        ],
        out_specs=pl.BlockSpec((tm, d), lambda i: (i, 0)),
        out_shape=jax.ShapeDtypeStruct((m, d), _F32),
        compiler_params=_params("parallel"),
        name="ple_final",
    )(h, p, norm_w, w_gate, w_proj, final_w)


def _pad_lanes(v, width):
    return jnp.pad(v.astype(_F32), (0, width - v.shape[0]))


def kernel(x, p, ffn1_norm, ffn1_w_gu, ffn1_w_down, mix_norm, w_in, dn_conv, dn_a_log, dn_dt_bias, dn_out_norm, w_out, ffn2_norm, ffn2_w_gu, ffn2_w_down, ple_norm, ple_w_gate, ple_w_proj, final_norm):
    batch, seq, d = x.shape
    depth = p.shape[0]
    m = batch * seq
    dn_w = DN_HEADS * HEAD_DIM
    gate_lo = 4 * dn_w
    gate_hi = gate_lo + 2 * DN_HEADS
    row = lambda v: v.reshape(1, -1).astype(_F32)

    h = x.reshape(m, d)
    for i in range(depth):
        h = _ffn(h, row(ffn1_norm[i]), ffn1_w_gu[i].astype(_BF16), ffn1_w_down[i].astype(_BF16),
                 tm=1024, tf=512)

        wi = w_in[i]
        sb_w = SB_HEADS * HEAD_DIM
        sb_scale = jnp.where(jnp.arange(3 * sb_w) < sb_w, (HEAD_DIM ** -0.5) * LOG2E, 1.0).astype(_F32)
        w_dn = wi[:, :gate_lo].astype(_BF16)
        w_sb = (wi[:, gate_hi:] * sb_scale[None, :]).astype(_BF16)
        w_gate = wi[:, gate_lo:gate_hi]
        w_gate_hi = w_gate.astype(_BF16)
        w_gate_lo = (w_gate - w_gate_hi.astype(_F32)).astype(_BF16)
        w_gate = jnp.pad(jnp.concatenate([w_gate_hi, w_gate_lo], axis=1), ((0, 0), (0, GATE_LANES - 4 * DN_HEADS)))
        p_heads, gates = _inproj(h, row(mix_norm[i]), w_dn, w_sb, w_gate, tm=1024, tn=1024)

        gate_par = jnp.stack([_pad_lanes(dn_a_log[i], GATE_LANES), _pad_lanes(dn_dt_bias[i], GATE_LANES)])
        conv_w = dn_conv[i].astype(_F32).reshape(CONV_K, 3 * DN_HEADS, HEAD_DIM).transpose(1, 0, 2)
        o_dn = _deltanet(p_heads, gates, conv_w, gate_par, row(dn_out_norm[i]), batch=batch, seq=seq)
        o_sb = _stickbreak(p_heads, batch=batch, seq=seq, head_off=4 * DN_HEADS)
        h = _outproj(h, o_dn, o_sb, w_out[i].astype(_BF16), tm=512)

        h = _ffn(h, row(ffn2_norm[i]), ffn2_w_gu[i].astype(_BF16), ffn2_w_down[i].astype(_BF16),
                 tm=1024, tf=512)
        assert i == depth - 1, "the final RMSNorm is fused into the last layer's embedding kernel"
        h = _ple(h, p[i].reshape(m, -1), row(ple_norm[i]), ple_w_gate[i].astype(_BF16),
                 ple_w_proj[i].astype(_BF16), row(final_norm), tm=512)
    return h.reshape(batch, seq, d)
```

```python
import functools
import math

import jax
import jax.numpy as jnp
from jax import lax
from jax.experimental import pallas as pl
from jax.experimental.pallas import tpu as pltpu

_F32 = jnp.float32
_BF16 = jnp.bfloat16

RMS_EPS = 1e-6
L2_EPS = 1e-6
DN_HEADS = 8
SB_HEADS = 8
HEAD_DIM = 128
CONV_K = 4
GATE_LANES = 128
CONV_PAD = 8

DN_CHUNK = 64
DN_GROUP = 4
DN_GROUP_ROWS = DN_CHUNK * DN_GROUP
DN_GROUPS_PER_ITER = 8
DN_HEADS_PER_STEP = 4
CUMSUM_PAD = DN_CHUNK // 2
DN_CHUNK_ROWS = HEAD_DIM + DN_CHUNK

SB_TQ = 256
SB_TK = 256
SB_HEADS_PER_STEP = 8

VMEM_LIMIT = 56 * 1024 * 1024
FFN_VMEM_LIMIT = 60 * 1024 * 1024
LOG2E = math.log2(math.e)


def _dot(a, b):
    return jnp.dot(a, b, preferred_element_type=_F32)


def _dot_nt(a, b):
    return lax.dot_general(a, b, (((1,), (1,)), ((), ())), preferred_element_type=_F32)


def _dot_tn(a, b):
    return lax.dot_general(a, b, (((0,), (0,)), ((), ())), preferred_element_type=_F32)


def _rms(x, w):
    return x * lax.rsqrt(jnp.mean(x * x, axis=-1, keepdims=True) + RMS_EPS) * w


def _softplus(x):
    return jnp.maximum(x, 0.0) + jnp.log1p(jnp.exp(-jnp.abs(x)))


def _silu(x):
    return x * jax.nn.sigmoid(x)


def _params(*sem, vmem_limit=VMEM_LIMIT):
    return pltpu.CompilerParams(dimension_semantics=sem, vmem_limit_bytes=vmem_limit)


def _ffn_kernel(x_ref, nw_ref, wg_ref, wu_ref, wd_ref, o_ref, n_ref):
    @pl.when(pl.program_id(1) == 0)
    def _():
        x = x_ref[...]
        n_ref[...] = _rms(x, nw_ref[...]).astype(_BF16)
        o_ref[...] = x

    n = n_ref[...]
    g = _dot(n, wg_ref[...])
    u = _dot(n, wu_ref[...])
    a = (_silu(g) * u).astype(_BF16)
    o_ref[...] += 0.5 * _dot(a, wd_ref[...].astype(_BF16))


def _ffn(h, norm_w, w_gu, w_down, *, tm, tf):
    m, d = h.shape
    f = w_down.shape[0]
    nf = f // tf
    return pl.pallas_call(
        _ffn_kernel,
        grid=(m // tm, nf),
        in_specs=[
            pl.BlockSpec((tm, d), lambda i, j: (i, 0)),
            pl.BlockSpec((1, d), lambda i, j: (0, 0)),
            pl.BlockSpec((d, tf), lambda i, j: (0, j)),
            pl.BlockSpec((d, tf), lambda i, j: (0, nf + j)),
            pl.BlockSpec((tf, d), lambda i, j: (j, 0)),
        ],
        out_specs=pl.BlockSpec((tm, d), lambda i, j: (i, 0)),
        out_shape=jax.ShapeDtypeStruct((m, d), _F32),
        scratch_shapes=[pltpu.VMEM((tm, d), _BF16)],
        compiler_params=_params("parallel", "arbitrary", vmem_limit=FFN_VMEM_LIMIT),
        name="ffn",
    )(h, norm_w, w_gu, w_gu, w_down)


def _inproj_kernel(na, x_ref, nw_ref, wa_ref, wb_ref, wg_ref, p_ref, gate_ref, n_ref):
    @pl.when(pl.program_id(1) == 0)
    def _():
        n = _rms(x_ref[...], nw_ref[...])
        n_hi = n.astype(_BF16)
        n_lo = (n - n_hi.astype(_F32)).astype(_BF16)
        n_ref[...] = n_hi
        s = _dot(n_hi, wg_ref[...]) + _dot(n_lo, wg_ref[...])
        lane = lax.broadcasted_iota(jnp.int32, s.shape, 1)
        folded = s + pltpu.roll(s, GATE_LANES - 2 * DN_HEADS, axis=1)
        gate_ref[...] = jnp.where(lane < 2 * DN_HEADS, folded, 0.0)

    def project(w_ref):
        res = _dot(n_ref[...], w_ref[...]).astype(_BF16)
        for c in range(p_ref.shape[0]):
            p_ref[c] = res[:, c * HEAD_DIM:(c + 1) * HEAD_DIM]

    @pl.when(pl.program_id(1) < na)
    def _():
        project(wa_ref)

    @pl.when(pl.program_id(1) >= na)
    def _():
        project(wb_ref)


def _inproj(h, norm_w, w_a, w_b, w_gate, *, tm, tn):
    m, d = h.shape
    na, nb = w_a.shape[1] // tn, w_b.shape[1] // tn
    n_main = w_a.shape[1] + w_b.shape[1]
    hpt = tn // HEAD_DIM
    return pl.pallas_call(
        functools.partial(_inproj_kernel, na),
        grid=(m // tm, na + nb),
        in_specs=[
            pl.BlockSpec((tm, d), lambda i, j: (i, 0)),
            pl.BlockSpec((1, d), lambda i, j: (0, 0)),
            pl.BlockSpec((d, tn), lambda i, j: (0, jnp.minimum(j, na - 1))),
            pl.BlockSpec((d, tn), lambda i, j: (0, jnp.where(j < na, nb - 1, j - na))),
            pl.BlockSpec((d, GATE_LANES), lambda i, j: (0, 0)),
        ],
        out_specs=[
            pl.BlockSpec((hpt, tm, HEAD_DIM), lambda i, j: (j, i, 0)),
            pl.BlockSpec((tm, GATE_LANES), lambda i, j: (i, 0)),
        ],
        out_shape=[
            jax.ShapeDtypeStruct((n_main // HEAD_DIM, m, HEAD_DIM), _BF16),
            jax.ShapeDtypeStruct((m, GATE_LANES), _F32),
        ],
        scratch_shapes=[pltpu.VMEM((tm, d), _BF16)],
        compiler_params=_params("parallel", "arbitrary"),
        name="in_proj",
    )(h, norm_w, w_a, w_b, w_gate)


def _dn_kernel(q_ref, k_ref, v_ref, z_ref, gate_ref, cq_ref, ck_ref, cv_ref, gpar_ref, onorm_ref,
               o_ref,
               xp_ref, qn_ref, kn_ref, vc_ref, gcb_ref, bb_ref, cs_ref, gsel_ref,
               kq_ref, no_ref, egl_ref):
    nh, t, hd = q_ref.shape
    c = DN_CHUNK
    gr = DN_GROUP_ROWS
    cr = DN_CHUNK_ROWS
    head0 = pl.program_id(1) * nh

    gates = gate_ref[...]
    gpar = gpar_ref[...]
    x = -jnp.exp(gpar[0:1, :]) * _softplus(gates + gpar[1:2, :])
    pos = lax.broadcasted_iota(jnp.int32, (t, GATE_LANES), 0) & (c - 1)
    cs_ref[0:CUMSUM_PAD, :] = jnp.zeros((CUMSUM_PAD, GATE_LANES), _F32)
    s = 1
    while s < c:
        cs_ref[CUMSUM_PAD:CUMSUM_PAD + t, :] = x
        x = x + jnp.where(pos >= s, cs_ref[CUMSUM_PAD - s:CUMSUM_PAD - s + t, :], 0.0)
        s *= 2
    lane = lax.broadcasted_iota(jnp.int32, (t, GATE_LANES), 1)
    comb = jnp.where(lane < DN_HEADS, x, jax.nn.sigmoid(gates))
    comb_hi = comb.astype(_BF16)
    gsel_ref[:, :GATE_LANES] = comb_hi
    gsel_ref[:, GATE_LANES:] = (comb - comb_hi.astype(_F32)).astype(_BF16)
    xp_ref[0:CONV_PAD, :] = jnp.zeros((CONV_PAD, hd), _F32)

    row = lax.broadcasted_iota(jnp.int32, (gr, gr), 0)
    col = lax.broadcasted_iota(jnp.int32, (gr, gr), 1)
    same = (row // c) == (col // c)
    eye = row == col
    lower_incl = same & (row >= col)
    strict = same & (row > col)
    eye_c = jnp.where(lax.broadcasted_iota(jnp.int32, (c, gr), 0)
                      == (lax.broadcasted_iota(jnp.int32, (c, gr), 1) & (c - 1)), 1.0, 0.0)

    def conv_silu(x_ref, w):
        xp_ref[CONV_PAD:CONV_PAD + t, :] = x_ref[...].astype(_F32)
        acc = w[CONV_K - 1:CONV_K, :] * xp_ref[CONV_PAD:CONV_PAD + t, :]
        for sh in range(1, CONV_K):
            acc = acc + w[CONV_K - 1 - sh:CONV_K - sh, :] * xp_ref[CONV_PAD - sh:CONV_PAD - sh + t, :]
        return _silu(acc)

    def l2n(y, scale):
        return y * (lax.rsqrt(jnp.sum(y * y, axis=-1, keepdims=True) + L2_EPS) * scale)

    def solve_groups(rows):
        n = range(len(rows))
        q = [qn_ref[pl.ds(r, gr), :] for r in rows]
        k = [kn_ref[pl.ds(r, gr), :] for r in rows]
        v = [vc_ref[pl.ds(r, gr), :] for r in rows]
        bb = [bb_ref[pl.ds(r, gr), :] for r in rows]
        gcb = [gcb_ref[pl.ds(r, gr), :] for r in rows]
        k16 = [k[i].astype(_BF16) for i in n]
        kb = [k[i] * bb[i] for i in n]
        kk = [_dot_nt(kb[i].astype(_BF16), k16[i]) for i in n]
        qk = [_dot_nt(q[i].astype(_BF16), k16[i]) for i in n]
        eg = [jnp.exp(gcb[i]) for i in n]
        lmat, attn, sol = [], [], []
        for i in n:
            gc_sq = jnp.concatenate([gcb[i]] * (gr // hd), axis=-1)
            gc_row = jnp.sum(jnp.where(eye, gc_sq, 0.0), axis=0, keepdims=True)
            decay = jnp.exp(jnp.where(lower_incl, gc_sq - gc_row, -jnp.inf))
            lmat.append(jnp.where(strict, kk[i] * decay, 0.0))
            attn.append((qk[i] * decay).astype(_BF16))
            sol.append(jnp.concatenate([v[i] * bb[i], kb[i] * eg[i]], axis=-1).astype(_BF16))

        def compact(mat):
            out = mat[0:c]
            for j in range(1, DN_GROUP):
                out = out + mat[j * c:(j + 1) * c]
            return out

        def blockdiag(mat):
            return jnp.where(same, jnp.concatenate([mat] * DN_GROUP, axis=0), 0.0).astype(_BF16)

        pw = [compact(lmat[i]) for i in n]
        tinv = [eye_c - pw[i] for i in n]
        bd = [lmat[i].astype(_BF16) for i in n]
        m = 2
        while m < c:
            pw = [_dot(pw[i].astype(_BF16), bd[i]) for i in n]
            bd = [blockdiag(pw[i]) for i in n]
            tinv = [tinv[i] + _dot(tinv[i].astype(_BF16), bd[i]) for i in n]
            m *= 2
        sol = [_dot(blockdiag(tinv[i]), sol[i]).astype(_BF16) for i in n]
        auw = [_dot(attn[i], sol[i]) for i in n]
        kd = []
        for i in n:
            g_last = jnp.concatenate(
                [jnp.broadcast_to(gcb[i][(j + 1) * c - 1:(j + 1) * c, :], (c, hd)) for j in range(DN_GROUP)], axis=0)
            kd.append((k[i] * jnp.exp(g_last - gcb[i])).astype(_BF16))
        kuw = [[_dot_tn(kd[i][j * c:(j + 1) * c], sol[i][j * c:(j + 1) * c]) for j in range(DN_GROUP)] for i in n]
        res = []
        for i in n:
            qp16 = (q[i] * eg[i] - auw[i][:, hd:]).astype(_BF16)
            o0_16 = auw[i][:, :hd].astype(_BF16)
            res.append((qp16, o0_16, [m_.astype(_BF16) for m_ in kuw[i]], eg[i]))
        return res

    def store_group(hl, r, qp16, o0_16, kuw16, eg):
        for j in range(DN_GROUP):
            lo, hi = j * c, (j + 1) * c
            base = pl.multiple_of((r + lo) // c * cr, cr)
            kq_ref[hl, pl.ds(base, hd), :] = kuw16[j][:, hd:]
            kq_ref[hl, pl.ds(base + hd, c), :] = qp16[lo:hi]
            no_ref[hl, pl.ds(base, hd), :] = kuw16[j][:, :hd]
            no_ref[hl, pl.ds(base + hd, c), :] = o0_16[lo:hi]
            r8 = pl.multiple_of((r + lo) // (c // 8), 8)
            egl_ref[hl, pl.ds(r8, 8), :] = jnp.broadcast_to(eg[hi - 1:hi, :], (8, hd))

    def head_prologue(hl, carry):
        head = head0 + hl
        krow = lax.broadcasted_iota(jnp.int32, (2 * GATE_LANES, 2 * hd), 0) & (GATE_LANES - 1)
        ncol = lax.broadcasted_iota(jnp.int32, (2 * GATE_LANES, 2 * hd), 1)
        sel = jnp.where(krow == jnp.where(ncol < hd, head, head + DN_HEADS), 1.0, 0.0).astype(_BF16)
        picked = _dot(gsel_ref[...], sel)
        gcb_ref[...] = picked[:, :hd]
        bb_ref[...] = picked[:, hd:]
        qn_ref[...] = l2n(conv_silu(q_ref.at[hl], cq_ref[hl]), hd ** -0.5)
        kn_ref[...] = l2n(conv_silu(k_ref.at[hl], ck_ref[hl]), 1.0)
        vc_ref[...] = conv_silu(v_ref.at[hl], cv_ref[hl])

        def groups(gi, cc):
            rows = [pl.multiple_of((gi * DN_GROUPS_PER_ITER + sub) * gr, gr) for sub in range(DN_GROUPS_PER_ITER)]
            solved = solve_groups(rows)
            for r, res in zip(rows, solved):
                store_group(hl, r, *res)
            return cc

        lax.fori_loop(0, t // (gr * DN_GROUPS_PER_ITER), groups, 0)
        return carry

    lax.fori_loop(0, nh, head_prologue, 0)

    onorm = onorm_ref[...]

    def chunk(ci, states):
        r = pl.multiple_of(ci * c, c)
        rc = pl.multiple_of(ci * cr, cr)
        r8 = pl.multiple_of(ci * 8, 8)
        heads = range(nh)
        kq = [kq_ref[hl, pl.ds(rc, cr), :] for hl in heads]
        no = [no_ref[hl, pl.ds(rc, cr), :] for hl in heads]
        e_last = [egl_ref[hl, pl.ds(r8, 8), :][0:1] for hl in heads]
        z = [z_ref[hl, pl.ds(r, c), :] for hl in heads]
        ks = [_dot(kq[hl], states[hl].astype(_BF16)) for hl in heads]
        new_states = tuple(states[hl] * e_last[hl] - ks[hl][:hd] + no[hl][:hd].astype(_F32) for hl in heads)
        outs = []
        for hl in heads:
            o = ks[hl][hd:] + no[hl][hd:].astype(_F32)
            y = o * lax.rsqrt(jnp.mean(o * o, axis=-1, keepdims=True) + RMS_EPS) * onorm
            outs.append((y * _silu(z[hl].astype(_F32))).astype(_BF16))
        for hl in heads:
            o_ref[pl.ds(r, c), hl * hd:(hl + 1) * hd] = outs[hl]
        return new_states

    lax.fori_loop(0, t // c, chunk, tuple(jnp.zeros((hd, hd), _F32) for _ in range(nh)))


def _deltanet(p_heads, gates, conv_w, gate_par, out_norm, *, batch, seq):
    hd = HEAD_DIM
    nh = DN_HEADS_PER_STEP
    ng = DN_HEADS // nh
    c = DN_CHUNK
    assert seq % (DN_GROUP_ROWS * DN_GROUPS_PER_ITER) == 0, seq
    heads = lambda part: pl.BlockSpec((nh, seq, hd), lambda b, g: (part * ng + g, b, 0))
    convw = lambda part: pl.BlockSpec((nh, CONV_K, hd), lambda b, g: (part * ng + g, 0, 0))
    f32_rows = lambda: pltpu.VMEM((seq, hd), _F32)
    return pl.pallas_call(
        _dn_kernel,
        grid=(batch, ng),
        in_specs=[
            heads(0), heads(1), heads(2), heads(3),
            pl.BlockSpec((seq, GATE_LANES), lambda b, g: (b, 0)),
            convw(0), convw(1), convw(2),
            pl.BlockSpec((2, GATE_LANES), lambda b, g: (0, 0)),
            pl.BlockSpec((1, hd), lambda b, g: (0, 0)),
        ],
        out_specs=pl.BlockSpec((seq, nh * hd), lambda b, g: (b, g)),
        out_shape=jax.ShapeDtypeStruct((batch * seq, DN_HEADS * hd), _BF16),
        scratch_shapes=[
            pltpu.VMEM((CONV_PAD + seq, hd), _F32),
            f32_rows(), f32_rows(), f32_rows(),
            f32_rows(), f32_rows(),
            pltpu.VMEM((CUMSUM_PAD + seq, GATE_LANES), _F32),
            pltpu.VMEM((seq, 2 * GATE_LANES), _BF16),
            pltpu.VMEM((nh, seq // c * DN_CHUNK_ROWS, hd), _BF16),
            pltpu.VMEM((nh, seq // c * DN_CHUNK_ROWS, hd), _BF16),
            pltpu.VMEM((nh, seq // c * 8, hd), _F32),
        ],
        compiler_params=_params("parallel", "arbitrary"),
        name="deltanet",
    )(p_heads, p_heads, p_heads, p_heads, gates, conv_w, conv_w, conv_w, gate_par, out_norm)


def _sb_kernel(q_ref, k_ref, v_ref, o_ref):
    nh, tq, hd = q_ref.shape
    tk = SB_TK
    qi = pl.program_id(2)
    row = lax.broadcasted_iota(jnp.int32, (tq, tk), 0)
    col = lax.broadcasted_iota(jnp.int32, (tq, tk), 1)
    causal = col < row
    ntri = jnp.where(row >= col, -1.0, 0.0).astype(_BF16)

    def tiles(kt, nt, state, masked):
        heads = range(nh)
        steps = range(nt)
        ks = [pl.multiple_of((kt - j) * tk, tk) for j in steps]
        k_ = [[k_ref[h, pl.ds(ks[j], tk), :] for h in heads] for j in steps]
        v_ = [[v_ref[h, pl.ds(ks[j], tk), :] for h in heads] for j in steps]
        z2 = [[_dot_nt(q_ref[h], k_[j][h]) for h in heads] for j in steps]
        revs = []
        for j in steps:
            revs.append([])
            for h in heads:
                neg_abs = pltpu.bitcast(pltpu.bitcast(z2[j][h], jnp.uint32) | jnp.uint32(0x80000000), _F32)
                x = jnp.maximum(z2[j][h], 0.0) + jnp.log2(1.0 + jnp.exp2(neg_abs))
                if masked:
                    x = jnp.where(causal, x, 0.0)
                revs[j].append(_dot(x.astype(_BF16), ntri))
        out = []
        for h in heads:
            acc, carry = state[2 * h], state[2 * h + 1]
            for j in steps:
                a = jnp.exp2(z2[j][h] + revs[j][h] + jnp.concatenate([carry] * (tk // hd), axis=-1))
                if masked:
                    a = jnp.where(causal, a, 0.0)
                acc = acc + _dot(a.astype(_BF16), v_[j][h])
                carry = carry + jnp.broadcast_to(revs[j][h][:, 0:1], (tq, hd))
            out += [acc, carry]
        return tuple(out)

    zeros = jnp.zeros((tq, hd), _F32)
    state = tiles(qi, 1, (zeros,) * (2 * nh), True)
    odd = qi % 2
    state = lax.fori_loop(0, odd, lambda i, st: tiles(qi - 1, 1, st, False), state)
    state = lax.fori_loop(0, qi // 2, lambda i, st: tiles(qi - 1 - odd - 2 * i, 2, st, False), state)
    for h in range(nh):
        o_ref[:, h * hd:(h + 1) * hd] = state[2 * h].astype(_BF16)


def _stickbreak(p_heads, *, batch, seq, head_off):
    hd = HEAD_DIM
    nh = SB_HEADS_PER_STEP
    ng = SB_HEADS // nh
    nq = seq // SB_TQ
    assert SB_TQ == SB_TK and seq % SB_TQ == 0, seq
    base = head_off // nh
    return pl.pallas_call(
        _sb_kernel,
        grid=(batch, ng, nq),
        in_specs=[
            pl.BlockSpec((nh, SB_TQ, hd), lambda b, g, i: (base + g, b * nq + i, 0)),
            pl.BlockSpec((nh, seq, hd), lambda b, g, i: (base + ng + g, b, 0)),
            pl.BlockSpec((nh, seq, hd), lambda b, g, i: (base + 2 * ng + g, b, 0)),
        ],
        out_specs=pl.BlockSpec((SB_TQ, nh * hd), lambda b, g, i: (b * nq + i, g)),
        out_shape=jax.ShapeDtypeStruct((batch * seq, SB_HEADS * hd), _BF16),
        compiler_params=_params("parallel", "parallel", "arbitrary"),
        name="stickbreak",
    )(p_heads, p_heads, p_heads)


def _outproj_kernel(h_ref, a_ref, b_ref, wa_ref, wb_ref, o_ref):
    o_ref[...] = h_ref[...] + _dot(a_ref[...], wa_ref[...]) + _dot(b_ref[...], wb_ref[...])


def _outproj(h, o_dn, o_sb, w_out, *, tm):
    m, d = h.shape
    ka = o_dn.shape[1]
    kb = o_sb.shape[1]
    assert ka == kb
    return pl.pallas_call(
        _outproj_kernel,
        grid=(m // tm,),
        in_specs=[
            pl.BlockSpec((tm, d), lambda i: (i, 0)),
            pl.BlockSpec((tm, ka), lambda i: (i, 0)),
            pl.BlockSpec((tm, kb), lambda i: (i, 0)),
            pl.BlockSpec((ka, d), lambda i: (0, 0)),
            pl.BlockSpec((kb, d), lambda i: (1, 0)),
        ],
        out_specs=pl.BlockSpec((tm, d), lambda i: (i, 0)),
        out_shape=jax.ShapeDtypeStruct((m, d), _F32),
        compiler_params=_params("parallel"),
        name="out_proj",
    )(h, o_dn, o_sb, w_out, w_out)


def _ple_kernel(h_ref, p_ref, nw_ref, wg_ref, wp_ref, fw_ref, o_ref):
    h = h_ref[...]
    n = _rms(h, nw_ref[...]).astype(_BF16)
    gate = jax.nn.sigmoid(_dot(n, wg_ref[...]))
    h = h + gate * _dot(p_ref[...].astype(_BF16), wp_ref[...])
    o_ref[...] = _rms(h, fw_ref[...])


def _ple(h, p, norm_w, w_gate, w_proj, final_w, *, tm):
    m, d = h.shape
    pd = p.shape[1]
    return pl.pallas_call(
        _ple_kernel,
        grid=(m // tm,),
        in_specs=[
            pl.BlockSpec((tm, d), lambda i: (i, 0)),
            pl.BlockSpec((tm, pd), lambda i: (i, 0)),
            pl.BlockSpec((1, d), lambda i: (0, 0)),
            pl.BlockSpec((d, d), lambda i: (0, 0)),
            pl.BlockSpec((pd, d), lambda i: (0, 0)),
            pl.BlockSpec((1, d), lambda i: (0, 0)),
        ],
        out_specs=pl.BlockSpec((tm, d), lambda i: (i, 0)),
        out_shape=jax.ShapeDtypeStruct((m, d), _F32),
        compiler_params=_params("parallel"),
        name="ple_final",
    )(h, p, norm_w, w_gate, w_proj, final_w)


def _pad_lanes(v, width):
    return jnp.pad(v.astype(_F32), (0, width - v.shape[0]))


def kernel(x, p, ffn1_norm, ffn1_w_gu, ffn1_w_down, mix_norm, w_in, dn_conv, dn_a_log, dn_dt_bias, dn_out_norm, w_out, ffn2_norm, ffn2_w_gu, ffn2_w_down, ple_norm, ple_w_gate, ple_w_proj, final_norm):
    batch, seq, d = x.shape
    depth = p.shape[0]
    m = batch * seq
    dn_w = DN_HEADS * HEAD_DIM
    gate_lo = 4 * dn_w
    gate_hi = gate_lo + 2 * DN_HEADS
    row = lambda v: v.reshape(1, -1).astype(_F32)

    h = x.reshape(m, d)
    for i in range(depth):
        h = _ffn(h, row(ffn1_norm[i]), ffn1_w_gu[i].astype(_BF16), ffn1_w_down[i].astype(_F32),
                 tm=1024, tf=512)

        wi = w_in[i]
        sb_w = SB_HEADS * HEAD_DIM
        sb_scale = jnp.where(jnp.arange(3 * sb_w) < sb_w, (HEAD_DIM ** -0.5) * LOG2E, 1.0).astype(_F32)
        w_dn = wi[:, :gate_lo].astype(_BF16)
        w_sb = (wi[:, gate_hi:] * sb_scale[None, :]).astype(_BF16)
        w_gate = wi[:, gate_lo:gate_hi]
        w_gate_hi = w_gate.astype(_BF16)
        w_gate_lo = (w_gate - w_gate_hi.astype(_F32)).astype(_BF16)
        w_gate = jnp.pad(jnp.concatenate([w_gate_hi, w_gate_lo], axis=1), ((0, 0), (0, GATE_LANES - 4 * DN_HEADS)))
        p_heads, gates = _inproj(h, row(mix_norm[i]), w_dn, w_sb, w_gate, tm=1024, tn=1024)

        gate_par = jnp.stack([_pad_lanes(dn_a_log[i], GATE_LANES), _pad_lanes(dn_dt_bias[i], GATE_LANES)])
        conv_w = dn_conv[i].astype(_F32).reshape(CONV_K, 3 * DN_HEADS, HEAD_DIM).transpose(1, 0, 2)
        o_dn = _deltanet(p_heads, gates, conv_w, gate_par, row(dn_out_norm[i]), batch=batch, seq=seq)
        o_sb = _stickbreak(p_heads, batch=batch, seq=seq, head_off=4 * DN_HEADS)
        h = _outproj(h, o_dn, o_sb, w_out[i].astype(_BF16), tm=512)

        h = _ffn(h, row(ffn2_norm[i]), ffn2_w_gu[i].astype(_BF16), ffn2_w_down[i].astype(_F32),
                 tm=1024, tf=512)
        assert i == depth - 1, "the final RMSNorm is fused into the last layer's embedding kernel"
        h = _ple(h, p[i].reshape(m, -1), row(ple_norm[i]), ple_w_gate[i].astype(_BF16),
                 ple_w_proj[i].astype(_BF16), row(final_norm), tm=512)
    return h.reshape(batch, seq, d)
```

```python
import functools
import math

import jax
import jax.numpy as jnp
from jax import lax
from jax.experimental import pallas as pl
from jax.experimental.pallas import tpu as pltpu

_F32 = jnp.float32
_BF16 = jnp.bfloat16

RMS_EPS = 1e-6
L2_EPS = 1e-6
DN_HEADS = 8
SB_HEADS = 8
HEAD_DIM = 128
CONV_K = 4
GATE_LANES = 128
CONV_PAD = 8

DN_CHUNK = 64
DN_GROUP = 4
DN_GROUP_ROWS = DN_CHUNK * DN_GROUP
DN_GROUPS_PER_ITER = 8
DN_HEADS_PER_STEP = 4
CUMSUM_PAD = DN_CHUNK // 2
DN_CHUNK_ROWS = HEAD_DIM + DN_CHUNK

SB_TQ = 256
SB_TK = 256
SB_HEADS_PER_STEP = 8

VMEM_LIMIT = 56 * 1024 * 1024
FFN_VMEM_LIMIT = 60 * 1024 * 1024
LOG2E = math.log2(math.e)


def _dot(a, b):
    return jnp.dot(a, b, preferred_element_type=_F32)


def _dot_nt(a, b):
    return lax.dot_general(a, b, (((1,), (1,)), ((), ())), preferred_element_type=_F32)


def _dot_tn(a, b):
    return lax.dot_general(a, b, (((0,), (0,)), ((), ())), preferred_element_type=_F32)


def _rms(x, w):
    return x * lax.rsqrt(jnp.mean(x * x, axis=-1, keepdims=True) + RMS_EPS) * w


def _softplus(x):
    return jnp.maximum(x, 0.0) + jnp.log1p(jnp.exp(-jnp.abs(x)))


def _silu(x):
    half = 0.5 * x
    return half + half * jnp.tanh(half)


def _params(*sem, vmem_limit=VMEM_LIMIT):
    return pltpu.CompilerParams(dimension_semantics=sem, vmem_limit_bytes=vmem_limit)


def _ffn_kernel(x_ref, nw_ref, wg_ref, wu_ref, wd_ref, o_ref, n_ref):
    @pl.when(pl.program_id(1) == 0)
    def _():
        x = x_ref[...]
        n_ref[...] = _rms(x, nw_ref[...]).astype(_BF16)
        o_ref[...] = x

    n = n_ref[...]
    g = _dot(n, wg_ref[...])
    u = _dot(n, wu_ref[...])
    a = (_silu(g) * u).astype(_BF16)
    o_ref[...] += 0.5 * _dot(a, wd_ref[...].astype(_BF16))


def _ffn(h, norm_w, w_gu, w_down, *, tm, tf):
    m, d = h.shape
    f = w_down.shape[0]
    nf = f // tf
    return pl.pallas_call(
        _ffn_kernel,
        grid=(m // tm, nf),
        in_specs=[
            pl.BlockSpec((tm, d), lambda i, j: (i, 0)),
            pl.BlockSpec((1, d), lambda i, j: (0, 0)),
            pl.BlockSpec((d, tf), lambda i, j: (0, j)),
            pl.BlockSpec((d, tf), lambda i, j: (0, nf + j)),
            pl.BlockSpec((tf, d), lambda i, j: (j, 0)),
        ],
        out_specs=pl.BlockSpec((tm, d), lambda i, j: (i, 0)),
        out_shape=jax.ShapeDtypeStruct((m, d), _F32),
        scratch_shapes=[pltpu.VMEM((tm, d), _BF16)],
        compiler_params=_params("parallel", "arbitrary", vmem_limit=FFN_VMEM_LIMIT),
        name="ffn",
    )(h, norm_w, w_gu, w_gu, w_down)


def _inproj_kernel(na, x_ref, nw_ref, wa_ref, wb_ref, wg_ref, p_ref, gate_ref, n_ref):
    @pl.when(pl.program_id(1) == 0)
    def _():
        n = _rms(x_ref[...], nw_ref[...])
        n_hi = n.astype(_BF16)
        n_lo = (n - n_hi.astype(_F32)).astype(_BF16)
        n_ref[...] = n_hi
        s = _dot_nt(n_hi, wg_ref[...]) + _dot_nt(n_lo, wg_ref[...])
        lane = lax.broadcasted_iota(jnp.int32, s.shape, 1)
        folded = s + pltpu.roll(s, GATE_LANES - 2 * DN_HEADS, axis=1)
        gate_ref[...] = jnp.where(lane < 2 * DN_HEADS, folded, 0.0)

    def project(w_ref):
        res = _dot_nt(n_ref[...], w_ref[...]).astype(_BF16)
        for c in range(p_ref.shape[0]):
            p_ref[c] = res[:, c * HEAD_DIM:(c + 1) * HEAD_DIM]

    @pl.when(pl.program_id(1) < na)
    def _():
        project(wa_ref)

    @pl.when(pl.program_id(1) >= na)
    def _():
        project(wb_ref)


def _inproj(h, norm_w, w_a, w_b, w_gate, *, tm, tn):
    m, d = h.shape
    na, nb = w_a.shape[0] // tn, w_b.shape[0] // tn
    n_main = w_a.shape[0] + w_b.shape[0]
    hpt = tn // HEAD_DIM
    return pl.pallas_call(
        functools.partial(_inproj_kernel, na),
        grid=(m // tm, na + nb),
        in_specs=[
            pl.BlockSpec((tm, d), lambda i, j: (i, 0)),
            pl.BlockSpec((1, d), lambda i, j: (0, 0)),
            pl.BlockSpec((tn, d), lambda i, j: (jnp.minimum(j, na - 1), 0)),
            pl.BlockSpec((tn, d), lambda i, j: (jnp.where(j < na, nb - 1, j - na), 0)),
            pl.BlockSpec((GATE_LANES, d), lambda i, j: (0, 0)),
        ],
        out_specs=[
            pl.BlockSpec((hpt, tm, HEAD_DIM), lambda i, j: (j, i, 0)),
            pl.BlockSpec((tm, GATE_LANES), lambda i, j: (i, 0)),
        ],
        out_shape=[
            jax.ShapeDtypeStruct((n_main // HEAD_DIM, m, HEAD_DIM), _BF16),
            jax.ShapeDtypeStruct((m, GATE_LANES), _F32),
        ],
        scratch_shapes=[pltpu.VMEM((tm, d), _BF16)],
        compiler_params=_params("parallel", "arbitrary"),
        name="in_proj",
    )(h, norm_w, w_a, w_b, w_gate)


def _dn_kernel(q_ref, k_ref, v_ref, z_ref, gate_ref, cq_ref, ck_ref, cv_ref, gpar_ref, onorm_ref,
               o_ref,
               xp_ref, qn_ref, kn_ref, vc_ref, gcb_ref, bb_ref, cs_ref, gsel_ref,
               kq_ref, no_ref, egl_ref):
    nh, t, hd = q_ref.shape
    c = DN_CHUNK
    gr = DN_GROUP_ROWS
    cr = DN_CHUNK_ROWS
    head0 = pl.program_id(1) * nh

    gates = gate_ref[...]
    gpar = gpar_ref[...]
    x = -jnp.exp(gpar[0:1, :]) * _softplus(gates + gpar[1:2, :])
    pos = lax.broadcasted_iota(jnp.int32, (t, GATE_LANES), 0) & (c - 1)
    cs_ref[0:CUMSUM_PAD, :] = jnp.zeros((CUMSUM_PAD, GATE_LANES), _F32)
    s = 1
    while s < c:
        cs_ref[CUMSUM_PAD:CUMSUM_PAD + t, :] = x
        x = x + jnp.where(pos >= s, cs_ref[CUMSUM_PAD - s:CUMSUM_PAD - s + t, :], 0.0)
        s *= 2
    lane = lax.broadcasted_iota(jnp.int32, (t, GATE_LANES), 1)
    comb = jnp.where(lane < DN_HEADS, x, jax.nn.sigmoid(gates))
    comb_hi = comb.astype(_BF16)
    gsel_ref[:, :GATE_LANES] = comb_hi
    gsel_ref[:, GATE_LANES:] = (comb - comb_hi.astype(_F32)).astype(_BF16)
    xp_ref[0:CONV_PAD, :] = jnp.zeros((CONV_PAD, hd), _F32)

    row = lax.broadcasted_iota(jnp.int32, (gr, gr), 0)
    col = lax.broadcasted_iota(jnp.int32, (gr, gr), 1)
    same = (row // c) == (col // c)
    eye = row == col
    lower_incl = same & (row >= col)
    strict = same & (row > col)
    same16 = jnp.where(same, 1.0, 0.0).astype(_BF16)
    eye_c = jnp.where(lax.broadcasted_iota(jnp.int32, (c, gr), 0)
                      == (lax.broadcasted_iota(jnp.int32, (c, gr), 1) & (c - 1)), 1.0, 0.0)

    def conv_silu(x_ref, w):
        xp_ref[CONV_PAD:CONV_PAD + t, :] = x_ref[...].astype(_F32)
        acc = w[CONV_K - 1:CONV_K, :] * xp_ref[CONV_PAD:CONV_PAD + t, :]
        for sh in range(1, CONV_K):
            acc = acc + w[CONV_K - 1 - sh:CONV_K - sh, :] * xp_ref[CONV_PAD - sh:CONV_PAD - sh + t, :]
        return _silu(acc)

    def l2n(y, scale):
        return y * (lax.rsqrt(jnp.sum(y * y, axis=-1, keepdims=True) + L2_EPS) * scale)

    def solve_groups(rows):
        n = range(len(rows))
        q = [qn_ref[pl.ds(r, gr), :] for r in rows]
        k = [kn_ref[pl.ds(r, gr), :] for r in rows]
        v = [vc_ref[pl.ds(r, gr), :] for r in rows]
        bb = [bb_ref[pl.ds(r, gr), :] for r in rows]
        gcb = [gcb_ref[pl.ds(r, gr), :] for r in rows]
        k16 = [k[i].astype(_BF16) for i in n]
        kb = [k[i] * bb[i] for i in n]
        kk = [_dot_nt(kb[i].astype(_BF16), k16[i]) for i in n]
        qk = [_dot_nt(q[i].astype(_BF16), k16[i]) for i in n]
        eg = [jnp.exp(gcb[i]) for i in n]
        lmat, attn, sol = [], [], []
        for i in n:
            gc_sq = jnp.concatenate([gcb[i]] * (gr // hd), axis=-1)
            gc_row = jnp.sum(jnp.where(eye, gc_sq, 0.0), axis=0, keepdims=True)
            decay = jnp.exp(jnp.where(lower_incl, gc_sq - gc_row, -jnp.inf))
            lmat.append(jnp.where(strict, kk[i] * decay, 0.0))
            attn.append((qk[i] * decay).astype(_BF16))
            sol.append(jnp.concatenate([v[i] * bb[i], kb[i] * eg[i]], axis=-1).astype(_BF16))

        def compact(mat):
            out = mat[0:c]
            for j in range(1, DN_GROUP):
                out = out + mat[j * c:(j + 1) * c]
            return out

        def blockdiag(mat):
            return jnp.concatenate([mat.astype(_BF16)] * DN_GROUP, axis=0) * same16

        pw = [compact(lmat[i]) for i in n]
        tinv = [eye_c - pw[i] for i in n]
        bd = [lmat[i].astype(_BF16) for i in n]
        m = 2
        while m < c:
            pw = [_dot(pw[i].astype(_BF16), bd[i]) for i in n]
            bd = [blockdiag(pw[i]) for i in n]
            tinv = [tinv[i] + _dot(tinv[i].astype(_BF16), bd[i]) for i in n]
            m *= 2
        sol = [_dot(blockdiag(tinv[i]), sol[i]).astype(_BF16) for i in n]
        auw = [_dot(attn[i], sol[i]) for i in n]
        kd = []
        for i in n:
            g_last = jnp.concatenate(
                [jnp.broadcast_to(gcb[i][(j + 1) * c - 1:(j + 1) * c, :], (c, hd)) for j in range(DN_GROUP)], axis=0)
            kd.append((k[i] * jnp.exp(g_last - gcb[i])).astype(_BF16))
        kuw = [[_dot_tn(kd[i][j * c:(j + 1) * c], sol[i][j * c:(j + 1) * c]) for j in range(DN_GROUP)] for i in n]
        res = []
        for i in n:
            qp16 = (q[i] * eg[i] - auw[i][:, hd:]).astype(_BF16)
            o0_16 = auw[i][:, :hd].astype(_BF16)
            res.append((qp16, o0_16, [m_.astype(_BF16) for m_ in kuw[i]], eg[i]))
        return res

    def store_group(hl, r, qp16, o0_16, kuw16, eg):
        for j in range(DN_GROUP):
            lo, hi = j * c, (j + 1) * c
            base = pl.multiple_of((r + lo) // c * cr, cr)
            kq_ref[hl, pl.ds(base, hd), :] = kuw16[j][:, hd:]
            kq_ref[hl, pl.ds(base + hd, c), :] = qp16[lo:hi]
            no_ref[hl, pl.ds(base, hd), :] = kuw16[j][:, :hd]
            no_ref[hl, pl.ds(base + hd, c), :] = o0_16[lo:hi]
            r8 = pl.multiple_of((r + lo) // (c // 8), 8)
            egl_ref[hl, pl.ds(r8, 8), :] = jnp.broadcast_to(eg[hi - 1:hi, :], (8, hd))

    def head_prologue(hl, carry):
        head = head0 + hl
        krow = lax.broadcasted_iota(jnp.int32, (2 * GATE_LANES, 2 * hd), 0) & (GATE_LANES - 1)
        ncol = lax.broadcasted_iota(jnp.int32, (2 * GATE_LANES, 2 * hd), 1)
        sel = jnp.where(krow == jnp.where(ncol < hd, head, head + DN_HEADS), 1.0, 0.0).astype(_BF16)
        picked = _dot(gsel_ref[...], sel)
        gcb_ref[...] = picked[:, :hd]
        bb_ref[...] = picked[:, hd:]
        qn_ref[...] = l2n(conv_silu(q_ref.at[hl], cq_ref[hl]), hd ** -0.5)
        kn_ref[...] = l2n(conv_silu(k_ref.at[hl], ck_ref[hl]), 1.0)
        vc_ref[...] = conv_silu(v_ref.at[hl], cv_ref[hl])

        def groups(gi, cc):
            rows = [pl.multiple_of((gi * DN_GROUPS_PER_ITER + sub) * gr, gr) for sub in range(DN_GROUPS_PER_ITER)]
            solved = solve_groups(rows)
            for r, res in zip(rows, solved):
                store_group(hl, r, *res)
            return cc

        lax.fori_loop(0, t // (gr * DN_GROUPS_PER_ITER), groups, 0)
        return carry

    lax.fori_loop(0, nh, head_prologue, 0)

    onorm = onorm_ref[...]

    def chunk(ci, states):
        r = pl.multiple_of(ci * c, c)
        rc = pl.multiple_of(ci * cr, cr)
        r8 = pl.multiple_of(ci * 8, 8)
        heads = range(nh)
        kq = [kq_ref[hl, pl.ds(rc, cr), :] for hl in heads]
        no = [no_ref[hl, pl.ds(rc, cr), :] for hl in heads]
        e_last = [egl_ref[hl, pl.ds(r8, 8), :][0:1] for hl in heads]
        z = [z_ref[hl, pl.ds(r, c), :] for hl in heads]
        ks = [_dot(kq[hl], states[hl].astype(_BF16)) for hl in heads]
        new_states = tuple(states[hl] * e_last[hl] - ks[hl][:hd] + no[hl][:hd].astype(_F32) for hl in heads)
        outs = []
        for hl in heads:
            o = ks[hl][hd:] + no[hl][hd:].astype(_F32)
            y = o * lax.rsqrt(jnp.mean(o * o, axis=-1, keepdims=True) + RMS_EPS) * onorm
            outs.append((y * _silu(z[hl].astype(_F32))).astype(_BF16))
        for hl in heads:
            o_ref[pl.ds(r, c), hl * hd:(hl + 1) * hd] = outs[hl]
        return new_states

    lax.fori_loop(0, t // c, chunk, tuple(jnp.zeros((hd, hd), _F32) for _ in range(nh)))


def _deltanet(p_heads, gates, conv_w, gate_par, out_norm, *, batch, seq):
    hd = HEAD_DIM
    nh = DN_HEADS_PER_STEP
    ng = DN_HEADS // nh
    c = DN_CHUNK
    assert seq % (DN_GROUP_ROWS * DN_GROUPS_PER_ITER) == 0, seq
    heads = lambda part: pl.BlockSpec((nh, seq, hd), lambda b, g: (part * ng + g, b, 0))
    convw = lambda part: pl.BlockSpec((nh, CONV_K, hd), lambda b, g: (part * ng + g, 0, 0))
    f32_rows = lambda: pltpu.VMEM((seq, hd), _F32)
    return pl.pallas_call(
        _dn_kernel,
        grid=(batch, ng),
        in_specs=[
            heads(0), heads(1), heads(2), heads(3),
            pl.BlockSpec((seq, GATE_LANES), lambda b, g: (b, 0)),
            convw(0), convw(1), convw(2),
            pl.BlockSpec((2, GATE_LANES), lambda b, g: (0, 0)),
            pl.BlockSpec((1, hd), lambda b, g: (0, 0)),
        ],
        out_specs=pl.BlockSpec((seq, nh * hd), lambda b, g: (b, g)),
        out_shape=jax.ShapeDtypeStruct((batch * seq, DN_HEADS * hd), _BF16),
        scratch_shapes=[
            pltpu.VMEM((CONV_PAD + seq, hd), _F32),
            f32_rows(), f32_rows(), f32_rows(),
            f32_rows(), f32_rows(),
            pltpu.VMEM((CUMSUM_PAD + seq, GATE_LANES), _F32),
            pltpu.VMEM((seq, 2 * GATE_LANES), _BF16),
            pltpu.VMEM((nh, seq // c * DN_CHUNK_ROWS, hd), _BF16),
            pltpu.VMEM((nh, seq // c * DN_CHUNK_ROWS, hd), _BF16),
            pltpu.VMEM((nh, seq // c * 8, hd), _F32),
        ],
        compiler_params=_params("parallel", "arbitrary"),
        name="deltanet",
    )(p_heads, p_heads, p_heads, p_heads, gates, conv_w, conv_w, conv_w, gate_par, out_norm)


def _sb_kernel(q_ref, k_ref, v_ref, o_ref):
    nh, tq, hd = q_ref.shape
    tk = SB_TK
    qi = pl.program_id(2)
    row = lax.broadcasted_iota(jnp.int32, (tq, tk), 0)
    col = lax.broadcasted_iota(jnp.int32, (tq, tk), 1)
    causal = col < row
    ntri = jnp.where(row >= col, -1.0, 0.0).astype(_BF16)

    def tiles(kt, nt, state, masked):
        heads = range(nh)
        steps = range(nt)
        ks = [pl.multiple_of((kt - j) * tk, tk) for j in steps]
        k_ = [[k_ref[h, pl.ds(ks[j], tk), :] for h in heads] for j in steps]
        v_ = [[v_ref[h, pl.ds(ks[j], tk), :] for h in heads] for j in steps]
        z2 = [[_dot_nt(q_ref[h], k_[j][h]) for h in heads] for j in steps]
        revs = []
        for j in steps:
            revs.append([])
            for h in heads:
                neg_abs = pltpu.bitcast(pltpu.bitcast(z2[j][h], jnp.uint32) | jnp.uint32(0x80000000), _F32)
                x = jnp.maximum(z2[j][h], 0.0) + jnp.log2(1.0 + jnp.exp2(neg_abs))
                if masked:
                    x = jnp.where(causal, x, 0.0)
                revs[j].append(_dot(x.astype(_BF16), ntri))
        out = []
        for h in heads:
            acc, carry = state[2 * h], state[2 * h + 1]
            for j in steps:
                a = jnp.exp2(z2[j][h] + revs[j][h] + jnp.concatenate([carry] * (tk // hd), axis=-1))
                if masked:
                    a = jnp.where(causal, a, 0.0)
                acc = acc + _dot(a.astype(_BF16), v_[j][h])
                carry = carry + jnp.broadcast_to(revs[j][h][:, 0:1], (tq, hd))
            out += [acc, carry]
        return tuple(out)

    zeros = jnp.zeros((tq, hd), _F32)
    state = tiles(qi, 1, (zeros,) * (2 * nh), True)
    odd = qi % 2
    state = lax.fori_loop(0, odd, lambda i, st: tiles(qi - 1, 1, st, False), state)
    state = lax.fori_loop(0, qi // 2, lambda i, st: tiles(qi - 1 - odd - 2 * i, 2, st, False), state)
    for h in range(nh):
        o_ref[:, h * hd:(h + 1) * hd] = state[2 * h].astype(_BF16)


def _stickbreak(p_heads, *, batch, seq, head_off):
    hd = HEAD_DIM
    nh = SB_HEADS_PER_STEP
    ng = SB_HEADS // nh
    nq = seq // SB_TQ
    assert SB_TQ == SB_TK and seq % SB_TQ == 0, seq
    base = head_off // nh
    return pl.pallas_call(
        _sb_kernel,
        grid=(batch, ng, nq),
        in_specs=[
            pl.BlockSpec((nh, SB_TQ, hd), lambda b, g, i: (base + g, b * nq + i, 0)),
            pl.BlockSpec((nh, seq, hd), lambda b, g, i: (base + ng + g, b, 0)),
            pl.BlockSpec((nh, seq, hd), lambda b, g, i: (base + 2 * ng + g, b, 0)),
        ],
        out_specs=pl.BlockSpec((SB_TQ, nh * hd), lambda b, g, i: (b * nq + i, g)),
        out_shape=jax.ShapeDtypeStruct((batch * seq, SB_HEADS * hd), _BF16),
        compiler_params=_params("parallel", "parallel", "arbitrary"),
        name="stickbreak",
    )(p_heads, p_heads, p_heads)


def _outproj_kernel(h_ref, a_ref, b_ref, wa_ref, wb_ref, o_ref):
    o_ref[...] = h_ref[...] + _dot(a_ref[...], wa_ref[...]) + _dot(b_ref[...], wb_ref[...])


def _outproj(h, o_dn, o_sb, w_out, *, tm):
    m, d = h.shape
    ka = o_dn.shape[1]
    kb = o_sb.shape[1]
    assert ka == kb
    return pl.pallas_call(
        _outproj_kernel,
        grid=(m // tm,),
        in_specs=[
            pl.BlockSpec((tm, d), lambda i: (i, 0)),
            pl.BlockSpec((tm, ka), lambda i: (i, 0)),
            pl.BlockSpec((tm, kb), lambda i: (i, 0)),
            pl.BlockSpec((ka, d), lambda i: (0, 0)),
            pl.BlockSpec((kb, d), lambda i: (1, 0)),
        ],
        out_specs=pl.BlockSpec((tm, d), lambda i: (i, 0)),
        out_shape=jax.ShapeDtypeStruct((m, d), _F32),
        compiler_params=_params("parallel"),
        name="out_proj",
    )(h, o_dn, o_sb, w_out, w_out)


def _ple_kernel(h_ref, p_ref, nw_ref, wg_ref, wp_ref, fw_ref, o_ref):
    rows = h_ref.shape[0] // 2
    parts = [pl.ds(i * rows, rows) for i in range(2)]
    hs = [h_ref[r, :] for r in parts]
    ns = [_rms(h, nw_ref[...]).astype(_BF16) for h in hs]
    gs = [_dot(n, wg_ref[...]) for n in ns]
    es = [_dot(p_ref[r, :].astype(_BF16), wp_ref[...]) for r in parts]
    for r, h, g, e in zip(parts, hs, gs, es):
        o_ref[r, :] = _rms(h + jax.nn.sigmoid(g) * e, fw_ref[...])


def _ple(h, p, norm_w, w_gate, w_proj, final_w, *, tm):
    m, d = h.shape
    pd = p.shape[1]
    return pl.pallas_call(
        _ple_kernel,
        grid=(m // tm,),
        in_specs=[
            pl.BlockSpec((tm, d), lambda i: (i, 0)),
            pl.BlockSpec((tm, pd), lambda i: (i, 0)),
            pl.BlockSpec((1, d), lambda i: (0, 0)),
            pl.BlockSpec((d, d), lambda i: (0, 0)),
            pl.BlockSpec((pd, d), lambda i: (0, 0)),
            pl.BlockSpec((1, d), lambda i: (0, 0)),
        ],
        out_specs=pl.BlockSpec((tm, d), lambda i: (i, 0)),
        out_shape=jax.ShapeDtypeStruct((m, d), _F32),
        compiler_params=_params("parallel"),
        name="ple_final",
    )(h, p, norm_w, w_gate, w_proj, final_w)


def _pad_lanes(v, width):
    return jnp.pad(v.astype(_F32), (0, width - v.shape[0]))


def kernel(x, p, ffn1_norm, ffn1_w_gu, ffn1_w_down, mix_norm, w_in, dn_conv, dn_a_log, dn_dt_bias, dn_out_norm, w_out, ffn2_norm, ffn2_w_gu, ffn2_w_down, ple_norm, ple_w_gate, ple_w_proj, final_norm):
    batch, seq, d = x.shape
    depth = p.shape[0]
    m = batch * seq
    dn_w = DN_HEADS * HEAD_DIM
    gate_lo = 4 * dn_w
    gate_hi = gate_lo + 2 * DN_HEADS
    row = lambda v: v.reshape(1, -1).astype(_F32)

    h = x.reshape(m, d)
    for i in range(depth):
        h = _ffn(h, row(ffn1_norm[i]), ffn1_w_gu[i].astype(_BF16), ffn1_w_down[i].astype(_F32),
                 tm=1024, tf=512)

        wt = jnp.swapaxes(w_in[i], 0, 1)
        sb_w = SB_HEADS * HEAD_DIM
        sb_scale = jnp.where(jnp.arange(3 * sb_w) < sb_w, (HEAD_DIM ** -0.5) * LOG2E, 1.0).astype(_F32)
        w_dn = wt[:gate_lo].astype(_BF16)
        w_sb = (wt[gate_hi:] * sb_scale[:, None]).astype(_BF16)
        w_gate = wt[gate_lo:gate_hi]
        w_gate_hi = w_gate.astype(_BF16)
        w_gate_lo = (w_gate - w_gate_hi.astype(_F32)).astype(_BF16)
        w_gate = jnp.pad(jnp.concatenate([w_gate_hi, w_gate_lo], axis=0), ((0, GATE_LANES - 4 * DN_HEADS), (0, 0)))
        p_heads, gates = _inproj(h, row(mix_norm[i]), w_dn, w_sb, w_gate, tm=1024, tn=1024)

        gate_par = jnp.stack([_pad_lanes(dn_a_log[i], GATE_LANES), _pad_lanes(dn_dt_bias[i], GATE_LANES)])
        conv_w = dn_conv[i].astype(_F32).reshape(CONV_K, 3 * DN_HEADS, HEAD_DIM).transpose(1, 0, 2)
        o_dn = _deltanet(p_heads, gates, conv_w, gate_par, row(dn_out_norm[i]), batch=batch, seq=seq)
        o_sb = _stickbreak(p_heads, batch=batch, seq=seq, head_off=4 * DN_HEADS)
        h = _outproj(h, o_dn, o_sb, w_out[i].astype(_BF16), tm=512)

        h = _ffn(h, row(ffn2_norm[i]), ffn2_w_gu[i].astype(_BF16), ffn2_w_down[i].astype(_F32),
                 tm=1024, tf=512)
        assert i == depth - 1, "the final RMSNorm is fused into the last layer's embedding kernel"
        h = _ple(h, p[i].reshape(m, -1), row(ple_norm[i]), ple_w_gate[i].astype(_BF16),
                 ple_w_proj[i].astype(_BF16), row(final_norm), tm=512)
    return h.reshape(batch, seq, d)
```

```python
import functools
import math

import jax
import jax.numpy as jnp
from jax import lax
from jax.experimental import pallas as pl
from jax.experimental.pallas import tpu as pltpu

_F32 = jnp.float32
_BF16 = jnp.bfloat16

RMS_EPS = 1e-6
L2_EPS = 1e-6
DN_HEADS = 8
SB_HEADS = 8
HEAD_DIM = 128
CONV_K = 4
GATE_LANES = 128
CONV_PAD = 8

DN_CHUNK = 64
DN_GROUP = 4
DN_GROUP_ROWS = DN_CHUNK * DN_GROUP
DN_GROUPS_PER_ITER = 8
DN_HEADS_PER_STEP = 4
CUMSUM_PAD = DN_CHUNK // 2
DN_CHUNK_ROWS = HEAD_DIM + DN_CHUNK

SB_TQ = 256
SB_TK = 256
SB_HEADS_PER_STEP = 8

VMEM_LIMIT = 56 * 1024 * 1024
FFN_VMEM_LIMIT = 60 * 1024 * 1024
LOG2E = math.log2(math.e)


def _dot(a, b):
    return jnp.dot(a, b, preferred_element_type=_F32)


def _dot_nt(a, b):
    return lax.dot_general(a, b, (((1,), (1,)), ((), ())), preferred_element_type=_F32)


def _dot_tn(a, b):
    return lax.dot_general(a, b, (((0,), (0,)), ((), ())), preferred_element_type=_F32)


def _rms(x, w):
    return x * lax.rsqrt(jnp.mean(x * x, axis=-1, keepdims=True) + RMS_EPS) * w


def _softplus(x):
    return jnp.maximum(x, 0.0) + jnp.log1p(jnp.exp(-jnp.abs(x)))


def _silu(x):
    half = 0.5 * x
    return half + half * jnp.tanh(half)


def _params(*sem, vmem_limit=VMEM_LIMIT):
    return pltpu.CompilerParams(dimension_semantics=sem, vmem_limit_bytes=vmem_limit)


def _ffn_kernel(x_ref, nw_ref, wg_ref, wu_ref, wd_ref, o_ref, n_ref):
    @pl.when(pl.program_id(1) == 0)
    def _():
        x = x_ref[...]
        n_ref[...] = _rms(x, nw_ref[...]).astype(_BF16)
        o_ref[...] = x

    n = n_ref[...]
    g = _dot(n, wg_ref[...])
    u = _dot(n, wu_ref[...])
    a = (_silu(g) * (0.5 * u)).astype(_BF16)
    o_ref[...] += _dot(a, wd_ref[...].astype(_BF16))


def _ffn(h, norm_w, w_gu, w_down, *, tm, tf):
    m, d = h.shape
    f = w_down.shape[0]
    nf = f // tf
    return pl.pallas_call(
        _ffn_kernel,
        grid=(m // tm, nf),
        in_specs=[
            pl.BlockSpec((tm, d), lambda i, j: (i, 0)),
            pl.BlockSpec((1, d), lambda i, j: (0, 0)),
            pl.BlockSpec((d, tf), lambda i, j: (0, j)),
            pl.BlockSpec((d, tf), lambda i, j: (0, nf + j)),
            pl.BlockSpec((tf, d), lambda i, j: (j, 0)),
        ],
        out_specs=pl.BlockSpec((tm, d), lambda i, j: (i, 0)),
        out_shape=jax.ShapeDtypeStruct((m, d), _F32),
        scratch_shapes=[pltpu.VMEM((tm, d), _BF16)],
        compiler_params=_params("parallel", "arbitrary", vmem_limit=FFN_VMEM_LIMIT),
        name="ffn",
    )(h, norm_w, w_gu, w_gu, w_down)


def _inproj_kernel(na, x_ref, nw_ref, wa_ref, wb_ref, wg_ref, p_ref, gate_ref, n_ref):
    @pl.when(pl.program_id(1) == 0)
    def _():
        n = _rms(x_ref[...], nw_ref[...])
        n_hi = n.astype(_BF16)
        n_lo = (n - n_hi.astype(_F32)).astype(_BF16)
        n_ref[...] = n_hi
        s = _dot_nt(n_hi, wg_ref[...]) + _dot_nt(n_lo, wg_ref[...])
        lane = lax.broadcasted_iota(jnp.int32, s.shape, 1)
        folded = s + pltpu.roll(s, GATE_LANES - 2 * DN_HEADS, axis=1)
        gate_ref[...] = jnp.where(lane < 2 * DN_HEADS, folded, 0.0)

    def project(w_ref):
        res = _dot_nt(n_ref[...], w_ref[...]).astype(_BF16)
        for c in range(p_ref.shape[0]):
            p_ref[c] = res[:, c * HEAD_DIM:(c + 1) * HEAD_DIM]

    @pl.when(pl.program_id(1) < na)
    def _():
        project(wa_ref)

    @pl.when(pl.program_id(1) >= na)
    def _():
        project(wb_ref)


def _inproj(h, norm_w, w_a, w_b, w_gate, *, tm, tn):
    m, d = h.shape
    na, nb = w_a.shape[0] // tn, w_b.shape[0] // tn
    n_main = w_a.shape[0] + w_b.shape[0]
    hpt = tn // HEAD_DIM
    return pl.pallas_call(
        functools.partial(_inproj_kernel, na),
        grid=(m // tm, na + nb),
        in_specs=[
            pl.BlockSpec((tm, d), lambda i, j: (i, 0)),
            pl.BlockSpec((1, d), lambda i, j: (0, 0)),
            pl.BlockSpec((tn, d), lambda i, j: (jnp.minimum(j, na - 1), 0)),
            pl.BlockSpec((tn, d), lambda i, j: (jnp.where(j < na, nb - 1, j - na), 0)),
            pl.BlockSpec((GATE_LANES, d), lambda i, j: (0, 0)),
        ],
        out_specs=[
            pl.BlockSpec((hpt, tm, HEAD_DIM), lambda i, j: (j, i, 0)),
            pl.BlockSpec((tm, GATE_LANES), lambda i, j: (i, 0)),
        ],
        out_shape=[
            jax.ShapeDtypeStruct((n_main // HEAD_DIM, m, HEAD_DIM), _BF16),
            jax.ShapeDtypeStruct((m, GATE_LANES), _F32),
        ],
        scratch_shapes=[pltpu.VMEM((tm, d), _BF16)],
        compiler_params=_params("parallel", "arbitrary"),
        name="in_proj",
    )(h, norm_w, w_a, w_b, w_gate)


def _dn_kernel(q_ref, k_ref, v_ref, z_ref, gate_ref, cq_ref, ck_ref, cv_ref, gpar_ref, onorm_ref,
               o_ref,
               xp_ref, qn_ref, kn_ref, vc_ref, gcb_ref, bb_ref, cs_ref, gsel_ref,
               kq_ref, no_ref, egl_ref):
    nh, t, hd = q_ref.shape
    c = DN_CHUNK
    gr = DN_GROUP_ROWS
    cr = DN_CHUNK_ROWS
    head0 = pl.program_id(1) * nh

    gates = gate_ref[...]
    gpar = gpar_ref[...]
    x = -jnp.exp(gpar[0:1, :]) * _softplus(gates + gpar[1:2, :])
    pos = lax.broadcasted_iota(jnp.int32, (t, GATE_LANES), 0) & (c - 1)
    cs_ref[0:CUMSUM_PAD, :] = jnp.zeros((CUMSUM_PAD, GATE_LANES), _F32)
    s = 1
    while s < c:
        cs_ref[CUMSUM_PAD:CUMSUM_PAD + t, :] = x
        x = x + jnp.where(pos >= s, cs_ref[CUMSUM_PAD - s:CUMSUM_PAD - s + t, :], 0.0)
        s *= 2
    lane = lax.broadcasted_iota(jnp.int32, (t, GATE_LANES), 1)
    comb = jnp.where(lane < DN_HEADS, x, jax.nn.sigmoid(gates))
    comb_hi = comb.astype(_BF16)
    gsel_ref[:, :GATE_LANES] = comb_hi
    gsel_ref[:, GATE_LANES:] = (comb - comb_hi.astype(_F32)).astype(_BF16)
    xp_ref[0:CONV_PAD, :] = jnp.zeros((CONV_PAD, hd), _F32)

    row = lax.broadcasted_iota(jnp.int32, (gr, gr), 0)
    col = lax.broadcasted_iota(jnp.int32, (gr, gr), 1)
    same = (row // c) == (col // c)
    eye = row == col
    lower_incl = same & (row >= col)
    strict = same & (row > col)
    same16 = jnp.where(same, 1.0, 0.0).astype(_BF16)
    eye_c = jnp.where(lax.broadcasted_iota(jnp.int32, (c, gr), 0)
                      == (lax.broadcasted_iota(jnp.int32, (c, gr), 1) & (c - 1)), 1.0, 0.0)

    def conv_silu(x_ref, w):
        xp_ref[CONV_PAD:CONV_PAD + t, :] = x_ref[...].astype(_F32)
        acc = w[CONV_K - 1:CONV_K, :] * xp_ref[CONV_PAD:CONV_PAD + t, :]
        for sh in range(1, CONV_K):
            acc = acc + w[CONV_K - 1 - sh:CONV_K - sh, :] * xp_ref[CONV_PAD - sh:CONV_PAD - sh + t, :]
        return _silu(acc)

    def l2n(y, scale):
        return y * (lax.rsqrt(jnp.sum(y * y, axis=-1, keepdims=True) + L2_EPS) * scale)

    def solve_groups(rows):
        n = range(len(rows))
        q = [qn_ref[pl.ds(r, gr), :] for r in rows]
        k = [kn_ref[pl.ds(r, gr), :] for r in rows]
        v = [vc_ref[pl.ds(r, gr), :] for r in rows]
        bb = [bb_ref[pl.ds(r, gr), :] for r in rows]
        gcb = [gcb_ref[pl.ds(r, gr), :] for r in rows]
        k16 = [k[i].astype(_BF16) for i in n]
        kb = [k[i] * bb[i] for i in n]
        kk = [_dot_nt(kb[i].astype(_BF16), k16[i]) for i in n]
        qk = [_dot_nt(q[i].astype(_BF16), k16[i]) for i in n]
        eg = [jnp.exp(gcb[i]) for i in n]
        lmat, attn, sol = [], [], []
        for i in n:
            gc_sq = jnp.concatenate([gcb[i]] * (gr // hd), axis=-1)
            gc_row = jnp.sum(jnp.where(eye, gc_sq, 0.0), axis=0, keepdims=True)
            decay = jnp.exp(jnp.where(lower_incl, gc_sq - gc_row, -jnp.inf))
            lmat.append(jnp.where(strict, kk[i] * decay, 0.0))
            attn.append((qk[i] * decay).astype(_BF16))
            sol.append(jnp.concatenate([v[i] * bb[i], kb[i] * eg[i]], axis=-1).astype(_BF16))

        def compact(mat):
            out = mat[0:c]
            for j in range(1, DN_GROUP):
                out = out + mat[j * c:(j + 1) * c]
            return out

        def blockdiag(mat):
            return jnp.concatenate([mat.astype(_BF16)] * DN_GROUP, axis=0) * same16

        pw = [compact(lmat[i]) for i in n]
        tinv = [eye_c - pw[i] for i in n]
        bd = [lmat[i].astype(_BF16) for i in n]
        m = 2
        while m < c:
            pw = [_dot(pw[i].astype(_BF16), bd[i]) for i in n]
            bd = [blockdiag(pw[i]) for i in n]
            tinv = [tinv[i] + _dot(tinv[i].astype(_BF16), bd[i]) for i in n]
            m *= 2
        sol = [_dot(blockdiag(tinv[i]), sol[i]).astype(_BF16) for i in n]
        auw = [_dot(attn[i], sol[i]) for i in n]
        kd = []
        for i in n:
            g_last = jnp.concatenate(
                [jnp.broadcast_to(gcb[i][(j + 1) * c - 1:(j + 1) * c, :], (c, hd)) for j in range(DN_GROUP)], axis=0)
            kd.append((k[i] * jnp.exp(g_last - gcb[i])).astype(_BF16))
        kuw = [[_dot_tn(kd[i][j * c:(j + 1) * c], sol[i][j * c:(j + 1) * c]) for j in range(DN_GROUP)] for i in n]
        res = []
        for i in n:
            qp16 = (q[i] * eg[i] - auw[i][:, hd:]).astype(_BF16)
            o0_16 = auw[i][:, :hd].astype(_BF16)
            res.append((qp16, o0_16, [m_.astype(_BF16) for m_ in kuw[i]], eg[i]))
        return res

    def store_group(hl, r, qp16, o0_16, kuw16, eg):
        for j in range(DN_GROUP):
            lo, hi = j * c, (j + 1) * c
            base = pl.multiple_of((r + lo) // c * cr, cr)
            kq_ref[hl, pl.ds(base, hd), :] = kuw16[j][:, hd:]
            kq_ref[hl, pl.ds(base + hd, c), :] = qp16[lo:hi]
            no_ref[hl, pl.ds(base, hd), :] = kuw16[j][:, :hd]
            no_ref[hl, pl.ds(base + hd, c), :] = o0_16[lo:hi]
            r8 = pl.multiple_of((r + lo) // (c // 8), 8)
            egl_ref[hl, pl.ds(r8, 8), :] = jnp.broadcast_to(eg[hi - 1:hi, :], (8, hd))

    def head_prologue(hl, carry):
        head = head0 + hl
        krow = lax.broadcasted_iota(jnp.int32, (2 * GATE_LANES, 2 * hd), 0) & (GATE_LANES - 1)
        ncol = lax.broadcasted_iota(jnp.int32, (2 * GATE_LANES, 2 * hd), 1)
        sel = jnp.where(krow == jnp.where(ncol < hd, head, head + DN_HEADS), 1.0, 0.0).astype(_BF16)
        picked = _dot(gsel_ref[...], sel)
        gcb_ref[...] = picked[:, :hd]
        bb_ref[...] = picked[:, hd:]
        qn_ref[...] = l2n(conv_silu(q_ref.at[hl], cq_ref[hl]), hd ** -0.5)
        kn_ref[...] = l2n(conv_silu(k_ref.at[hl], ck_ref[hl]), 1.0)
        vc_ref[...] = conv_silu(v_ref.at[hl], cv_ref[hl])

        def groups(gi, cc):
            rows = [pl.multiple_of((gi * DN_GROUPS_PER_ITER + sub) * gr, gr) for sub in range(DN_GROUPS_PER_ITER)]
            solved = solve_groups(rows)
            for r, res in zip(rows, solved):
                store_group(hl, r, *res)
            return cc

        lax.fori_loop(0, t // (gr * DN_GROUPS_PER_ITER), groups, 0)
        return carry

    lax.fori_loop(0, nh, head_prologue, 0)

    onorm = onorm_ref[...]

    def chunk(ci, states):
        r = pl.multiple_of(ci * c, c)
        rc = pl.multiple_of(ci * cr, cr)
        r8 = pl.multiple_of(ci * 8, 8)
        heads = range(nh)
        kq = [kq_ref[hl, pl.ds(rc, cr), :] for hl in heads]
        no = [no_ref[hl, pl.ds(rc, cr), :] for hl in heads]
        e_last = [egl_ref[hl, pl.ds(r8, 8), :][0:1] for hl in heads]
        z = [z_ref[hl, pl.ds(r, c), :] for hl in heads]
        ks = [_dot(kq[hl], states[hl].astype(_BF16)) for hl in heads]
        new_states = tuple(states[hl] * e_last[hl] - ks[hl][:hd] + no[hl][:hd].astype(_F32) for hl in heads)
        outs = []
        for hl in heads:
            o = ks[hl][hd:] + no[hl][hd:].astype(_F32)
            y = o * lax.rsqrt(jnp.mean(o * o, axis=-1, keepdims=True) + RMS_EPS) * onorm
            outs.append((y * _silu(z[hl].astype(_F32))).astype(_BF16))
        for hl in heads:
            o_ref[pl.ds(r, c), hl * hd:(hl + 1) * hd] = outs[hl]
        return new_states

    lax.fori_loop(0, t // c, chunk, tuple(jnp.zeros((hd, hd), _F32) for _ in range(nh)), unroll=8)


def _deltanet(p_heads, gates, conv_w, gate_par, out_norm, *, batch, seq):
    hd = HEAD_DIM
    nh = DN_HEADS_PER_STEP
    ng = DN_HEADS // nh
    c = DN_CHUNK
    assert seq % (DN_GROUP_ROWS * DN_GROUPS_PER_ITER) == 0, seq
    heads = lambda part: pl.BlockSpec((nh, seq, hd), lambda b, g: (part * ng + g, b, 0))
    convw = lambda part: pl.BlockSpec((nh, CONV_K, hd), lambda b, g: (part * ng + g, 0, 0))
    f32_rows = lambda: pltpu.VMEM((seq, hd), _F32)
    return pl.pallas_call(
        _dn_kernel,
        grid=(batch, ng),
        in_specs=[
            heads(0), heads(1), heads(2), heads(3),
            pl.BlockSpec((seq, GATE_LANES), lambda b, g: (b, 0)),
            convw(0), convw(1), convw(2),
            pl.BlockSpec((2, GATE_LANES), lambda b, g: (0, 0)),
            pl.BlockSpec((1, hd), lambda b, g: (0, 0)),
        ],
        out_specs=pl.BlockSpec((seq, nh * hd), lambda b, g: (b, g)),
        out_shape=jax.ShapeDtypeStruct((batch * seq, DN_HEADS * hd), _BF16),
        scratch_shapes=[
            pltpu.VMEM((CONV_PAD + seq, hd), _F32),
            f32_rows(), f32_rows(), f32_rows(),
            f32_rows(), f32_rows(),
            pltpu.VMEM((CUMSUM_PAD + seq, GATE_LANES), _F32),
            pltpu.VMEM((seq, 2 * GATE_LANES), _BF16),
            pltpu.VMEM((nh, seq // c * DN_CHUNK_ROWS, hd), _BF16),
            pltpu.VMEM((nh, seq // c * DN_CHUNK_ROWS, hd), _BF16),
            pltpu.VMEM((nh, seq // c * 8, hd), _F32),
        ],
        compiler_params=_params("parallel", "arbitrary"),
        name="deltanet",
    )(p_heads, p_heads, p_heads, p_heads, gates, conv_w, conv_w, conv_w, gate_par, out_norm)


def _sb_kernel(q_ref, k_ref, v_ref, o_ref):
    nh, tq, hd = q_ref.shape
    tk = SB_TK
    qi = pl.program_id(2)
    row = lax.broadcasted_iota(jnp.int32, (tq, tk), 0)
    col = lax.broadcasted_iota(jnp.int32, (tq, tk), 1)
    causal = col < row
    ntri = jnp.where(row >= col, -1.0, 0.0).astype(_BF16)

    def tiles(kt, nt, state, masked):
        heads = range(nh)
        steps = range(nt)
        ks = [pl.multiple_of((kt - j) * tk, tk) for j in steps]
        k_ = [[k_ref[h, pl.ds(ks[j], tk), :] for h in heads] for j in steps]
        v_ = [[v_ref[h, pl.ds(ks[j], tk), :] for h in heads] for j in steps]
        z2 = [[_dot_nt(q_ref[h], k_[j][h]) for h in heads] for j in steps]
        revs = []
        for j in steps:
            revs.append([])
            for h in heads:
                neg_abs = pltpu.bitcast(pltpu.bitcast(z2[j][h], jnp.uint32) | jnp.uint32(0x80000000), _F32)
                x = jnp.maximum(z2[j][h], 0.0) + jnp.log2(1.0 + jnp.exp2(neg_abs))
                if masked:
                    x = jnp.where(causal, x, 0.0)
                revs[j].append(_dot(x.astype(_BF16), ntri))
        out = []
        for h in heads:
            acc, carry = state[2 * h], state[2 * h + 1]
            for j in steps:
                a = jnp.exp2(z2[j][h] + revs[j][h] + jnp.concatenate([carry] * (tk // hd), axis=-1))
                if masked:
                    a = jnp.where(causal, a, 0.0)
                acc = acc + _dot(a.astype(_BF16), v_[j][h])
                carry = carry + jnp.broadcast_to(revs[j][h][:, 0:1], (tq, hd))
            out += [acc, carry]
        return tuple(out)

    zeros = jnp.zeros((tq, hd), _F32)
    state = tiles(qi, 1, (zeros,) * (2 * nh), True)
    odd = qi % 2
    state = lax.fori_loop(0, odd, lambda i, st: tiles(qi - 1, 1, st, False), state)
    state = lax.fori_loop(0, qi // 2, lambda i, st: tiles(qi - 1 - odd - 2 * i, 2, st, False), state)
    for h in range(nh):
        o_ref[:, h * hd:(h + 1) * hd] = state[2 * h].astype(_BF16)


def _stickbreak(p_heads, *, batch, seq, head_off):
    hd = HEAD_DIM
    nh = SB_HEADS_PER_STEP
    ng = SB_HEADS // nh
    nq = seq // SB_TQ
    assert SB_TQ == SB_TK and seq % SB_TQ == 0, seq
    base = head_off // nh
    return pl.pallas_call(
        _sb_kernel,
        grid=(batch, ng, nq),
        in_specs=[
            pl.BlockSpec((nh, SB_TQ, hd), lambda b, g, i: (base + g, b * nq + i, 0)),
            pl.BlockSpec((nh, seq, hd), lambda b, g, i: (base + ng + g, b, 0)),
            pl.BlockSpec((nh, seq, hd), lambda b, g, i: (base + 2 * ng + g, b, 0)),
        ],
        out_specs=pl.BlockSpec((SB_TQ, nh * hd), lambda b, g, i: (b * nq + i, g)),
        out_shape=jax.ShapeDtypeStruct((batch * seq, SB_HEADS * hd), _BF16),
        compiler_params=_params("parallel", "parallel", "arbitrary"),
        name="stickbreak",
    )(p_heads, p_heads, p_heads)


def _outproj_kernel(h_ref, a_ref, b_ref, wa_ref, wb_ref, o_ref):
    o_ref[...] = h_ref[...] + _dot(a_ref[...], wa_ref[...]) + _dot(b_ref[...], wb_ref[...])


def _outproj(h, o_dn, o_sb, w_out, *, tm):
    m, d = h.shape
    ka = o_dn.shape[1]
    kb = o_sb.shape[1]
    assert ka == kb
    return pl.pallas_call(
        _outproj_kernel,
        grid=(m // tm,),
        in_specs=[
            pl.BlockSpec((tm, d), lambda i: (i, 0)),
            pl.BlockSpec((tm, ka), lambda i: (i, 0)),
            pl.BlockSpec((tm, kb), lambda i: (i, 0)),
            pl.BlockSpec((ka, d), lambda i: (0, 0)),
            pl.BlockSpec((kb, d), lambda i: (1, 0)),
        ],
        out_specs=pl.BlockSpec((tm, d), lambda i: (i, 0)),
        out_shape=jax.ShapeDtypeStruct((m, d), _F32),
        compiler_params=_params("parallel"),
        name="out_proj",
    )(h, o_dn, o_sb, w_out, w_out)


def _ple_kernel(h_ref, p_ref, nw_ref, wg_ref, wp_ref, fw_ref, o_ref):
    rows = h_ref.shape[0] // 2
    parts = [pl.ds(i * rows, rows) for i in range(2)]
    hs = [h_ref[r, :] for r in parts]
    ns = [_rms(h, nw_ref[...]).astype(_BF16) for h in hs]
    gs = [_dot(n, wg_ref[...]) for n in ns]
    es = [_dot(p_ref[r, :].astype(_BF16), wp_ref[...]) for r in parts]
    for r, h, g, e in zip(parts, hs, gs, es):
        o_ref[r, :] = _rms(h + jax.nn.sigmoid(g) * e, fw_ref[...])


def _ple(h, p, norm_w, w_gate, w_proj, final_w, *, tm):
    m, d = h.shape
    pd = p.shape[1]
    return pl.pallas_call(
        _ple_kernel,
        grid=(m // tm,),
        in_specs=[
            pl.BlockSpec((tm, d), lambda i: (i, 0)),
            pl.BlockSpec((tm, pd), lambda i: (i, 0)),
            pl.BlockSpec((1, d), lambda i: (0, 0)),
            pl.BlockSpec((d, d), lambda i: (0, 0)),
            pl.BlockSpec((pd, d), lambda i: (0, 0)),
            pl.BlockSpec((1, d), lambda i: (0, 0)),
        ],
        out_specs=pl.BlockSpec((tm, d), lambda i: (i, 0)),
        out_shape=jax.ShapeDtypeStruct((m, d), _F32),
        compiler_params=_params("parallel"),
        name="ple_final",
    )(h, p, norm_w, w_gate, w_proj, final_w)


def _pad_lanes(v, width):
    return jnp.pad(v.astype(_F32), (0, width - v.shape[0]))


def kernel(x, p, ffn1_norm, ffn1_w_gu, ffn1_w_down, mix_norm, w_in, dn_conv, dn_a_log, dn_dt_bias, dn_out_norm, w_out, ffn2_norm, ffn2_w_gu, ffn2_w_down, ple_norm, ple_w_gate, ple_w_proj, final_norm):
    batch, seq, d = x.shape
    depth = p.shape[0]
    m = batch * seq
    dn_w = DN_HEADS * HEAD_DIM
    gate_lo = 4 * dn_w
    gate_hi = gate_lo + 2 * DN_HEADS
    row = lambda v: v.reshape(1, -1).astype(_F32)

    h = x.reshape(m, d)
    for i in range(depth):
        h = _ffn(h, row(ffn1_norm[i]), ffn1_w_gu[i].astype(_BF16), ffn1_w_down[i].astype(_F32),
                 tm=1024, tf=512)

        wt = jnp.swapaxes(w_in[i], 0, 1)
        sb_w = SB_HEADS * HEAD_DIM
        sb_scale = jnp.where(jnp.arange(3 * sb_w) < sb_w, (HEAD_DIM ** -0.5) * LOG2E, 1.0).astype(_F32)
        w_dn = wt[:gate_lo].astype(_BF16)
        w_sb = (wt[gate_hi:] * sb_scale[:, None]).astype(_BF16)
        w_gate = wt[gate_lo:gate_hi]
        w_gate_hi = w_gate.astype(_BF16)
        w_gate_lo = (w_gate - w_gate_hi.astype(_F32)).astype(_BF16)
        w_gate = jnp.pad(jnp.concatenate([w_gate_hi, w_gate_lo], axis=0), ((0, GATE_LANES - 4 * DN_HEADS), (0, 0)))
        p_heads, gates = _inproj(h, row(mix_norm[i]), w_dn, w_sb, w_gate, tm=1024, tn=1024)

        gate_par = jnp.stack([_pad_lanes(dn_a_log[i], GATE_LANES), _pad_lanes(dn_dt_bias[i], GATE_LANES)])
        conv_w = dn_conv[i].astype(_F32).reshape(CONV_K, 3 * DN_HEADS, HEAD_DIM).transpose(1, 0, 2)
        o_dn = _deltanet(p_heads, gates, conv_w, gate_par, row(dn_out_norm[i]), batch=batch, seq=seq)
        o_sb = _stickbreak(p_heads, batch=batch, seq=seq, head_off=4 * DN_HEADS)
        h = _outproj(h, o_dn, o_sb, w_out[i].astype(_BF16), tm=512)

        h = _ffn(h, row(ffn2_norm[i]), ffn2_w_gu[i].astype(_BF16), ffn2_w_down[i].astype(_F32),
                 tm=1024, tf=512)
        assert i == depth - 1, "the final RMSNorm is fused into the last layer's embedding kernel"
        h = _ple(h, p[i].reshape(m, -1), row(ple_norm[i]), ple_w_gate[i].astype(_BF16),
                 ple_w_proj[i].astype(_BF16), row(final_norm), tm=512)
    return h.reshape(batch, seq, d)
```

```python
import functools
import math

import jax
import jax.numpy as jnp
from jax import lax
from jax.experimental import pallas as pl
from jax.experimental.pallas import tpu as pltpu

_F32 = jnp.float32
_BF16 = jnp.bfloat16

RMS_EPS = 1e-6
L2_EPS = 1e-6
DN_HEADS = 8
SB_HEADS = 8
HEAD_DIM = 128
CONV_K = 4
GATE_LANES = 128
CONV_PAD = 8

DN_CHUNK = 64
DN_GROUP = 4
DN_GROUP_ROWS = DN_CHUNK * DN_GROUP
DN_GROUPS_PER_ITER = 8
DN_HEADS_PER_STEP = 4
CUMSUM_PAD = DN_CHUNK // 2
DN_CHUNK_ROWS = HEAD_DIM + DN_CHUNK

SB_TQ = 256
SB_TK = 256
SB_HEADS_PER_STEP = 8

VMEM_LIMIT = 56 * 1024 * 1024
FFN_VMEM_LIMIT = 60 * 1024 * 1024
LOG2E = math.log2(math.e)


def _dot(a, b):
    return jnp.dot(a, b, preferred_element_type=_F32)


def _dot_nt(a, b):
    return lax.dot_general(a, b, (((1,), (1,)), ((), ())), preferred_element_type=_F32)


def _dot_tn(a, b):
    return lax.dot_general(a, b, (((0,), (0,)), ((), ())), preferred_element_type=_F32)


def _rms(x, w):
    return x * lax.rsqrt(jnp.mean(x * x, axis=-1, keepdims=True) + RMS_EPS) * w


def _softplus(x):
    return jnp.maximum(x, 0.0) + jnp.log1p(jnp.exp(-jnp.abs(x)))


def _silu(x):
    half = 0.5 * x
    return half + half * jnp.tanh(half)


def _params(*sem, vmem_limit=VMEM_LIMIT):
    return pltpu.CompilerParams(dimension_semantics=sem, vmem_limit_bytes=vmem_limit)


def _ffn_kernel(x_ref, nw_ref, wg_ref, wu_ref, wd_ref, o_ref, n_ref):
    @pl.when(pl.program_id(1) == 0)
    def _():
        x = x_ref[...]
        n_ref[...] = _rms(x, nw_ref[...]).astype(_BF16)
        o_ref[...] = x

    n = n_ref[...]
    g = _dot(n, wg_ref[...])
    u = _dot(n, wu_ref[...])
    a = (_silu(g) * (0.5 * u)).astype(_BF16)
    o_ref[...] += _dot(a, wd_ref[...].astype(_BF16))


def _ffn(h, norm_w, w_gu, w_down, *, tm, tf):
    m, d = h.shape
    f = w_down.shape[0]
    nf = f // tf
    return pl.pallas_call(
        _ffn_kernel,
        grid=(m // tm, nf),
        in_specs=[
            pl.BlockSpec((tm, d), lambda i, j: (i, 0)),
            pl.BlockSpec((1, d), lambda i, j: (0, 0)),
            pl.BlockSpec((d, tf), lambda i, j: (0, j)),
            pl.BlockSpec((d, tf), lambda i, j: (0, nf + j)),
            pl.BlockSpec((tf, d), lambda i, j: (j, 0)),
        ],
        out_specs=pl.BlockSpec((tm, d), lambda i, j: (i, 0)),
        out_shape=jax.ShapeDtypeStruct((m, d), _F32),
        scratch_shapes=[pltpu.VMEM((tm, d), _BF16)],
        compiler_params=_params("parallel", "arbitrary", vmem_limit=FFN_VMEM_LIMIT),
        name="ffn",
    )(h, norm_w, w_gu, w_gu, w_down)


def _inproj_kernel(na, x_ref, nw_ref, wa_ref, wb_ref, wg_ref, p_ref, gate_ref, n_ref):
    @pl.when(pl.program_id(1) == 0)
    def _():
        n = _rms(x_ref[...], nw_ref[...])
        n_hi = n.astype(_BF16)
        n_lo = (n - n_hi.astype(_F32)).astype(_BF16)
        n_ref[...] = n_hi
        s = _dot_nt(n_hi, wg_ref[...]) + _dot_nt(n_lo, wg_ref[...])
        lane = lax.broadcasted_iota(jnp.int32, s.shape, 1)
        folded = s + pltpu.roll(s, GATE_LANES - 2 * DN_HEADS, axis=1)
        gate_ref[...] = jnp.where(lane < 2 * DN_HEADS, folded, 0.0)

    def project(w_ref):
        res = _dot_nt(n_ref[...], w_ref[...]).astype(_BF16)
        for c in range(p_ref.shape[0]):
            p_ref[c] = res[:, c * HEAD_DIM:(c + 1) * HEAD_DIM]

    @pl.when(pl.program_id(1) < na)
    def _():
        project(wa_ref)

    @pl.when(pl.program_id(1) >= na)
    def _():
        project(wb_ref)


def _inproj(h, norm_w, w_a, w_b, w_gate, *, tm, tn):
    m, d = h.shape
    na, nb = w_a.shape[0] // tn, w_b.shape[0] // tn
    n_main = w_a.shape[0] + w_b.shape[0]
    hpt = tn // HEAD_DIM
    return pl.pallas_call(
        functools.partial(_inproj_kernel, na),
        grid=(m // tm, na + nb),
        in_specs=[
            pl.BlockSpec((tm, d), lambda i, j: (i, 0)),
            pl.BlockSpec((1, d), lambda i, j: (0, 0)),
            pl.BlockSpec((tn, d), lambda i, j: (jnp.minimum(j, na - 1), 0)),
            pl.BlockSpec((tn, d), lambda i, j: (jnp.where(j < na, nb - 1, j - na), 0)),
            pl.BlockSpec((GATE_LANES, d), lambda i, j: (0, 0)),
        ],
        out_specs=[
            pl.BlockSpec((hpt, tm, HEAD_DIM), lambda i, j: (j, i, 0)),
            pl.BlockSpec((tm, GATE_LANES), lambda i, j: (i, 0)),
        ],
        out_shape=[
            jax.ShapeDtypeStruct((n_main // HEAD_DIM, m, HEAD_DIM), _BF16),
            jax.ShapeDtypeStruct((m, GATE_LANES), _F32),
        ],
        scratch_shapes=[pltpu.VMEM((tm, d), _BF16)],
        compiler_params=_params("parallel", "arbitrary"),
        name="in_proj",
    )(h, norm_w, w_a, w_b, w_gate)


def _dn_kernel(q_ref, k_ref, v_ref, z_ref, gate_ref, cq_ref, ck_ref, cv_ref, gpar_ref, onorm_ref,
               o_ref,
               xp_ref, qn_ref, kn_ref, vc_ref, gcb_ref, bb_ref, cs_ref, gsel_ref,
               kq_ref, no_ref, egl_ref):
    nh, t, hd = q_ref.shape
    c = DN_CHUNK
    gr = DN_GROUP_ROWS
    cr = DN_CHUNK_ROWS
    head0 = pl.program_id(1) * nh

    gates = gate_ref[...]
    gpar = gpar_ref[...]
    x = -jnp.exp(gpar[0:1, :]) * _softplus(gates + gpar[1:2, :])
    pos = lax.broadcasted_iota(jnp.int32, (t, GATE_LANES), 0) & (c - 1)
    cs_ref[0:CUMSUM_PAD, :] = jnp.zeros((CUMSUM_PAD, GATE_LANES), _F32)
    s = 1
    while s < c:
        cs_ref[CUMSUM_PAD:CUMSUM_PAD + t, :] = x
        x = x + jnp.where(pos >= s, cs_ref[CUMSUM_PAD - s:CUMSUM_PAD - s + t, :], 0.0)
        s *= 2
    lane = lax.broadcasted_iota(jnp.int32, (t, GATE_LANES), 1)
    comb = jnp.where(lane < DN_HEADS, x, jax.nn.sigmoid(gates))
    comb_hi = comb.astype(_BF16)
    gsel_ref[:, :GATE_LANES] = comb_hi
    gsel_ref[:, GATE_LANES:] = (comb - comb_hi.astype(_F32)).astype(_BF16)
    xp_ref[0:CONV_PAD, :] = jnp.zeros((CONV_PAD, hd), _F32)

    row = lax.broadcasted_iota(jnp.int32, (gr, gr), 0)
    col = lax.broadcasted_iota(jnp.int32, (gr, gr), 1)
    same = (row // c) == (col // c)
    eye = row == col
    lower_incl = same & (row >= col)
    strict = same & (row > col)
    same16 = jnp.where(same, 1.0, 0.0).astype(_BF16)
    eye_c = jnp.where(lax.broadcasted_iota(jnp.int32, (c, gr), 0)
                      == (lax.broadcasted_iota(jnp.int32, (c, gr), 1) & (c - 1)), 1.0, 0.0)

    def conv_silu(x_ref, w):
        xp_ref[CONV_PAD:CONV_PAD + t, :] = x_ref[...].astype(_F32)
        acc = w[CONV_K - 1:CONV_K, :] * xp_ref[CONV_PAD:CONV_PAD + t, :]
        for sh in range(1, CONV_K):
            acc = acc + w[CONV_K - 1 - sh:CONV_K - sh, :] * xp_ref[CONV_PAD - sh:CONV_PAD - sh + t, :]
        return _silu(acc)

    def l2n(y, scale):
        return y * (lax.rsqrt(jnp.sum(y * y, axis=-1, keepdims=True) + L2_EPS) * scale)

    def solve_groups(rows):
        n = range(len(rows))
        q = [qn_ref[pl.ds(r, gr), :] for r in rows]
        k = [kn_ref[pl.ds(r, gr), :] for r in rows]
        v = [vc_ref[pl.ds(r, gr), :] for r in rows]
        bb = [bb_ref[pl.ds(r, gr), :] for r in rows]
        gcb = [gcb_ref[pl.ds(r, gr), :] for r in rows]
        k16 = [k[i].astype(_BF16) for i in n]
        kb = [k[i] * bb[i] for i in n]
        kk = [_dot_nt(kb[i].astype(_BF16), k16[i]) for i in n]
        qk = [_dot_nt(q[i].astype(_BF16), k16[i]) for i in n]
        eg = [jnp.exp(gcb[i]) for i in n]
        lmat, attn, sol = [], [], []
        for i in n:
            gc_sq = jnp.concatenate([gcb[i]] * (gr // hd), axis=-1)
            gc_row = jnp.sum(jnp.where(eye, gc_sq, 0.0), axis=0, keepdims=True)
            decay = jnp.exp(jnp.where(lower_incl, gc_sq - gc_row, -jnp.inf))
            lmat.append(jnp.where(strict, kk[i] * decay, 0.0))
            attn.append((qk[i] * decay).astype(_BF16))
            sol.append(jnp.concatenate([v[i] * bb[i], kb[i] * eg[i]], axis=-1).astype(_BF16))

        def compact(mat):
            out = mat[0:c]
            for j in range(1, DN_GROUP):
                out = out + mat[j * c:(j + 1) * c]
            return out

        def blockdiag(mat):
            return jnp.concatenate([mat.astype(_BF16)] * DN_GROUP, axis=0) * same16

        pw = [compact(lmat[i]) for i in n]
        tinv = [eye_c - pw[i] for i in n]
        bd = [lmat[i].astype(_BF16) for i in n]
        m = 2
        while m < c:
            pw = [_dot(pw[i].astype(_BF16), bd[i]) for i in n]
            bd = [blockdiag(pw[i]) for i in n]
            tinv = [tinv[i] + _dot(tinv[i].astype(_BF16), bd[i]) for i in n]
            m *= 2
        sol = [_dot(blockdiag(tinv[i]), sol[i]).astype(_BF16) for i in n]
        auw = [_dot(attn[i], sol[i]) for i in n]
        kd = []
        for i in n:
            g_last = jnp.concatenate(
                [jnp.broadcast_to(gcb[i][(j + 1) * c - 1:(j + 1) * c, :], (c, hd)) for j in range(DN_GROUP)], axis=0)
            kd.append((k[i] * jnp.exp(g_last - gcb[i])).astype(_BF16))
        kuw = [[_dot_tn(kd[i][j * c:(j + 1) * c], sol[i][j * c:(j + 1) * c]) for j in range(DN_GROUP)] for i in n]
        res = []
        for i in n:
            qp16 = (q[i] * eg[i] - auw[i][:, hd:]).astype(_BF16)
            o0_16 = auw[i][:, :hd].astype(_BF16)
            res.append((qp16, o0_16, [m_.astype(_BF16) for m_ in kuw[i]], eg[i]))
        return res

    def store_group(hl, r, qp16, o0_16, kuw16, eg):
        for j in range(DN_GROUP):
            lo, hi = j * c, (j + 1) * c
            base = pl.multiple_of((r + lo) // c * cr, cr)
            kq_ref[hl, pl.ds(base, hd), :] = kuw16[j][:, hd:]
            kq_ref[hl, pl.ds(base + hd, c), :] = qp16[lo:hi]
            no_ref[hl, pl.ds(base, hd), :] = kuw16[j][:, :hd]
            no_ref[hl, pl.ds(base + hd, c), :] = o0_16[lo:hi]
            r8 = pl.multiple_of((r + lo) // (c // 8), 8)
            egl_ref[hl, pl.ds(r8, 8), :] = jnp.broadcast_to(eg[hi - 1:hi, :], (8, hd))

    def head_prologue(hl, carry):
        head = head0 + hl
        krow = lax.broadcasted_iota(jnp.int32, (2 * GATE_LANES, 2 * hd), 0) & (GATE_LANES - 1)
        ncol = lax.broadcasted_iota(jnp.int32, (2 * GATE_LANES, 2 * hd), 1)
        sel = jnp.where(krow == jnp.where(ncol < hd, head, head + DN_HEADS), 1.0, 0.0).astype(_BF16)
        picked = _dot(gsel_ref[...], sel)
        gcb_ref[...] = picked[:, :hd]
        bb_ref[...] = picked[:, hd:]
        qn_ref[...] = l2n(conv_silu(q_ref.at[hl], cq_ref[hl]), hd ** -0.5)
        kn_ref[...] = l2n(conv_silu(k_ref.at[hl], ck_ref[hl]), 1.0)
        vc_ref[...] = conv_silu(v_ref.at[hl], cv_ref[hl])

        def groups(gi, cc):
            rows = [pl.multiple_of((gi * DN_GROUPS_PER_ITER + sub) * gr, gr) for sub in range(DN_GROUPS_PER_ITER)]
            solved = solve_groups(rows)
            for r, res in zip(rows, solved):
                store_group(hl, r, *res)
            return cc

        lax.fori_loop(0, t // (gr * DN_GROUPS_PER_ITER), groups, 0)
        return carry

    lax.fori_loop(0, nh, head_prologue, 0, unroll=2)

    onorm = onorm_ref[...]

    def chunk(ci, states):
        r = pl.multiple_of(ci * c, c)
        rc = pl.multiple_of(ci * cr, cr)
        r8 = pl.multiple_of(ci * 8, 8)
        heads = range(nh)
        kq = [kq_ref[hl, pl.ds(rc, cr), :] for hl in heads]
        no = [no_ref[hl, pl.ds(rc, cr), :] for hl in heads]
        e_last = [egl_ref[hl, pl.ds(r8, 8), :][0:1] for hl in heads]
        z = [z_ref[hl, pl.ds(r, c), :] for hl in heads]
        ks = [_dot(kq[hl], states[hl].astype(_BF16)) for hl in heads]
        new_states = tuple(states[hl] * e_last[hl] - ks[hl][:hd] + no[hl][:hd].astype(_F32) for hl in heads)
        outs = []
        for hl in heads:
            o = ks[hl][hd:] + no[hl][hd:].astype(_F32)
            y = o * lax.rsqrt(jnp.mean(o * o, axis=-1, keepdims=True) + RMS_EPS) * onorm
            outs.append((y * _silu(z[hl].astype(_F32))).astype(_BF16))
        for hl in heads:
            o_ref[pl.ds(r, c), hl * hd:(hl + 1) * hd] = outs[hl]
        return new_states

    lax.fori_loop(0, t // c, chunk, tuple(jnp.zeros((hd, hd), _F32) for _ in range(nh)), unroll=8)


def _deltanet(p_heads, gates, conv_w, gate_par, out_norm, *, batch, seq):
    hd = HEAD_DIM
    nh = DN_HEADS_PER_STEP
    ng = DN_HEADS // nh
    c = DN_CHUNK
    assert seq % (DN_GROUP_ROWS * DN_GROUPS_PER_ITER) == 0, seq
    heads = lambda part: pl.BlockSpec((nh, seq, hd), lambda b, g: (part * ng + g, b, 0))
    convw = lambda part: pl.BlockSpec((nh, CONV_K, hd), lambda b, g: (part * ng + g, 0, 0))
    f32_rows = lambda: pltpu.VMEM((seq, hd), _F32)
    return pl.pallas_call(
        _dn_kernel,
        grid=(batch, ng),
        in_specs=[
            heads(0), heads(1), heads(2), heads(3),
            pl.BlockSpec((seq, GATE_LANES), lambda b, g: (b, 0)),
            convw(0), convw(1), convw(2),
            pl.BlockSpec((2, GATE_LANES), lambda b, g: (0, 0)),
            pl.BlockSpec((1, hd), lambda b, g: (0, 0)),
        ],
        out_specs=pl.BlockSpec((seq, nh * hd), lambda b, g: (b, g)),
        out_shape=jax.ShapeDtypeStruct((batch * seq, DN_HEADS * hd), _BF16),
        scratch_shapes=[
            pltpu.VMEM((CONV_PAD + seq, hd), _F32),
            f32_rows(), f32_rows(), f32_rows(),
            f32_rows(), f32_rows(),
            pltpu.VMEM((CUMSUM_PAD + seq, GATE_LANES), _F32),
            pltpu.VMEM((seq, 2 * GATE_LANES), _BF16),
            pltpu.VMEM((nh, seq // c * DN_CHUNK_ROWS, hd), _BF16),
            pltpu.VMEM((nh, seq // c * DN_CHUNK_ROWS, hd), _BF16),
            pltpu.VMEM((nh, seq // c * 8, hd), _F32),
        ],
        compiler_params=_params("parallel", "arbitrary"),
        name="deltanet",
    )(p_heads, p_heads, p_heads, p_heads, gates, conv_w, conv_w, conv_w, gate_par, out_norm)


def _sb_kernel(q_ref, k_ref, v_ref, o_ref, acc_ref, carry_ref):
    nh, tq, hd = q_ref.shape
    tk = SB_TK
    qi = pl.program_id(2)
    row = lax.broadcasted_iota(jnp.int32, (tq, tk), 0)
    col = lax.broadcasted_iota(jnp.int32, (tq, tk), 1)
    causal = col < row
    ntri = jnp.where(row >= col, -1.0, 0.0).astype(_BF16)

    def tiles(kt, nt, masked):
        heads = range(nh)
        steps = range(nt)
        ks = [pl.multiple_of((kt - j) * tk, tk) for j in steps]
        k_ = [[k_ref[h, pl.ds(ks[j], tk), :] for h in heads] for j in steps]
        v_ = [[v_ref[h, pl.ds(ks[j], tk), :] for h in heads] for j in steps]
        z2 = [[_dot_nt(q_ref[h], k_[j][h]) for h in heads] for j in steps]
        revs = []
        for j in steps:
            revs.append([])
            for h in heads:
                x = jnp.maximum(z2[j][h], 0.0) + jnp.log2(1.0 + jnp.exp2(-jnp.abs(z2[j][h])))
                if masked:
                    x = jnp.where(causal, x, 0.0)
                revs[j].append(_dot(x.astype(_BF16), ntri))
        out = []
        for h in heads:
            acc, carry = acc_ref[h], carry_ref[h]
            for j in steps:
                a = jnp.exp2(z2[j][h] + revs[j][h] + jnp.concatenate([carry] * (tk // hd), axis=-1))
                if masked:
                    a = jnp.where(causal, a, 0.0)
                acc = acc + _dot(a.astype(_BF16), v_[j][h])
                carry = carry + jnp.broadcast_to(revs[j][h][:, 0:1], (tq, hd))
            out.append((acc, carry))
        for h in heads:
            acc_ref[h], carry_ref[h] = out[h]

    def loop(count, body):
        lax.fori_loop(0, count, lambda i, c: (body(i), c)[1], 0)

    acc_ref[...] = jnp.zeros(acc_ref.shape, _F32)
    carry_ref[...] = jnp.zeros(carry_ref.shape, _F32)
    tiles(qi, 1, True)
    odd = qi % 2
    loop(odd, lambda i: tiles(qi - 1, 1, False))
    loop(qi // 2, lambda i: tiles(qi - 1 - odd - 2 * i, 2, False))
    for h in range(nh):
        o_ref[:, h * hd:(h + 1) * hd] = acc_ref[h].astype(_BF16)


def _stickbreak(p_heads, *, batch, seq, head_off):
    hd = HEAD_DIM
    nh = SB_HEADS_PER_STEP
    ng = SB_HEADS // nh
    nq = seq // SB_TQ
    assert SB_TQ == SB_TK and seq % SB_TQ == 0, seq
    base = head_off // nh
    return pl.pallas_call(
        _sb_kernel,
        grid=(batch, ng, nq),
        in_specs=[
            pl.BlockSpec((nh, SB_TQ, hd), lambda b, g, i: (base + g, b * nq + i, 0)),
            pl.BlockSpec((nh, seq, hd), lambda b, g, i: (base + ng + g, b, 0)),
            pl.BlockSpec((nh, seq, hd), lambda b, g, i: (base + 2 * ng + g, b, 0)),
        ],
        out_specs=pl.BlockSpec((SB_TQ, nh * hd), lambda b, g, i: (b * nq + i, g)),
        out_shape=jax.ShapeDtypeStruct((batch * seq, SB_HEADS * hd), _BF16),
        scratch_shapes=[
            pltpu.VMEM((nh, SB_TQ, hd), _F32),
            pltpu.VMEM((nh, SB_TQ, hd), _F32),
        ],
        compiler_params=_params("parallel", "parallel", "arbitrary"),
        name="stickbreak",
    )(p_heads, p_heads, p_heads)


def _outproj_kernel(h_ref, a_ref, b_ref, wa_ref, wb_ref, o_ref):
    o_ref[...] = h_ref[...] + _dot(a_ref[...], wa_ref[...]) + _dot(b_ref[...], wb_ref[...])


def _outproj(h, o_dn, o_sb, w_out, *, tm):
    m, d = h.shape
    ka = o_dn.shape[1]
    kb = o_sb.shape[1]
    assert ka == kb
    return pl.pallas_call(
        _outproj_kernel,
        grid=(m // tm,),
        in_specs=[
            pl.BlockSpec((tm, d), lambda i: (i, 0)),
            pl.BlockSpec((tm, ka), lambda i: (i, 0)),
            pl.BlockSpec((tm, kb), lambda i: (i, 0)),
            pl.BlockSpec((ka, d), lambda i: (0, 0)),
            pl.BlockSpec((kb, d), lambda i: (1, 0)),
        ],
        out_specs=pl.BlockSpec((tm, d), lambda i: (i, 0)),
        out_shape=jax.ShapeDtypeStruct((m, d), _F32),
        compiler_params=_params("parallel"),
        name="out_proj",
    )(h, o_dn, o_sb, w_out, w_out)


def _ple_kernel(h_ref, p_ref, nw_ref, wg_ref, wp_ref, fw_ref, o_ref):
    rows = h_ref.shape[0] // 2
    parts = [pl.ds(i * rows, rows) for i in range(2)]
    hs = [h_ref[r, :] for r in parts]
    ns = [_rms(h, nw_ref[...]).astype(_BF16) for h in hs]
    gs = [_dot(n, wg_ref[...]) for n in ns]
    es = [_dot(p_ref[r, :].astype(_BF16), wp_ref[...]) for r in parts]
    for r, h, g, e in zip(parts, hs, gs, es):
        o_ref[r, :] = _rms(h + jax.nn.sigmoid(g) * e, fw_ref[...])


def _ple(h, p, norm_w, w_gate, w_proj, final_w, *, tm):
    m, d = h.shape
    pd = p.shape[1]
    return pl.pallas_call(
        _ple_kernel,
        grid=(m // tm,),
        in_specs=[
            pl.BlockSpec((tm, d), lambda i: (i, 0)),
            pl.BlockSpec((tm, pd), lambda i: (i, 0)),
            pl.BlockSpec((1, d), lambda i: (0, 0)),
            pl.BlockSpec((d, d), lambda i: (0, 0)),
            pl.BlockSpec((pd, d), lambda i: (0, 0)),
            pl.BlockSpec((1, d), lambda i: (0, 0)),
        ],
        out_specs=pl.BlockSpec((tm, d), lambda i: (i, 0)),
        out_shape=jax.ShapeDtypeStruct((m, d), _F32),
        compiler_params=_params("parallel"),
        name="ple_final",
    )(h, p, norm_w, w_gate, w_proj, final_w)


def _pad_lanes(v, width):
    return jnp.pad(v.astype(_F32), (0, width - v.shape[0]))


def kernel(x, p, ffn1_norm, ffn1_w_gu, ffn1_w_down, mix_norm, w_in, dn_conv, dn_a_log, dn_dt_bias, dn_out_norm, w_out, ffn2_norm, ffn2_w_gu, ffn2_w_down, ple_norm, ple_w_gate, ple_w_proj, final_norm):
    batch, seq, d = x.shape
    depth = p.shape[0]
    m = batch * seq
    dn_w = DN_HEADS * HEAD_DIM
    gate_lo = 4 * dn_w
    gate_hi = gate_lo + 2 * DN_HEADS
    row = lambda v: v.reshape(1, -1).astype(_F32)

    h = x.reshape(m, d)
    for i in range(depth):
        h = _ffn(h, row(ffn1_norm[i]), ffn1_w_gu[i].astype(_BF16), ffn1_w_down[i].astype(_F32),
                 tm=1024, tf=512)

        wt = jnp.swapaxes(w_in[i], 0, 1)
        sb_w = SB_HEADS * HEAD_DIM
        sb_scale = jnp.where(jnp.arange(3 * sb_w) < sb_w, (HEAD_DIM ** -0.5) * LOG2E, 1.0).astype(_F32)
        w_dn = wt[:gate_lo].astype(_BF16)
        w_sb = (wt[gate_hi:] * sb_scale[:, None]).astype(_BF16)
        w_gate = wt[gate_lo:gate_hi]
        w_gate_hi = w_gate.astype(_BF16)
        w_gate_lo = (w_gate - w_gate_hi.astype(_F32)).astype(_BF16)
        w_gate = jnp.pad(jnp.concatenate([w_gate_hi, w_gate_lo], axis=0), ((0, GATE_LANES - 4 * DN_HEADS), (0, 0)))
        p_heads, gates = _inproj(h, row(mix_norm[i]), w_dn, w_sb, w_gate, tm=1024, tn=1024)

        gate_par = jnp.stack([_pad_lanes(dn_a_log[i], GATE_LANES), _pad_lanes(dn_dt_bias[i], GATE_LANES)])
        conv_w = dn_conv[i].astype(_F32).reshape(CONV_K, 3 * DN_HEADS, HEAD_DIM).transpose(1, 0, 2)
        o_dn = _deltanet(p_heads, gates, conv_w, gate_par, row(dn_out_norm[i]), batch=batch, seq=seq)
        o_sb = _stickbreak(p_heads, batch=batch, seq=seq, head_off=4 * DN_HEADS)
        h = _outproj(h, o_dn, o_sb, w_out[i].astype(_BF16), tm=512)

        h = _ffn(h, row(ffn2_norm[i]), ffn2_w_gu[i].astype(_BF16), ffn2_w_down[i].astype(_F32),
                 tm=1024, tf=512)
        assert i == depth - 1, "the final RMSNorm is fused into the last layer's embedding kernel"
        h = _ple(h, p[i].reshape(m, -1), row(ple_norm[i]), ple_w_gate[i].astype(_BF16),
                 ple_w_proj[i].astype(_BF16), row(final_norm), tm=512)
    return h.reshape(batch, seq, d)
```

```python
import functools
import math

import jax
import jax.numpy as jnp
from jax import lax
from jax.experimental import pallas as pl
from jax.experimental.pallas import tpu as pltpu

_F32 = jnp.float32
_BF16 = jnp.bfloat16

RMS_EPS = 1e-6
L2_EPS = 1e-6
DN_HEADS = 8
SB_HEADS = 8
HEAD_DIM = 128
CONV_K = 4
GATE_LANES = 128
CONV_PAD = 8

DN_CHUNK = 64
DN_GROUP = 4
DN_GROUP_ROWS = DN_CHUNK * DN_GROUP
DN_GROUPS_PER_ITER = 8
DN_HEADS_PER_STEP = 4
CUMSUM_PAD = DN_CHUNK // 2
DN_CHUNK_ROWS = HEAD_DIM + DN_CHUNK

SB_TQ = 256
SB_TK = 256
SB_HEADS_PER_STEP = 8

VMEM_LIMIT = 56 * 1024 * 1024
FFN_VMEM_LIMIT = 60 * 1024 * 1024
LOG2E = math.log2(math.e)


def _dot(a, b):
    return jnp.dot(a, b, preferred_element_type=_F32)


def _dot_nt(a, b):
    return lax.dot_general(a, b, (((1,), (1,)), ((), ())), preferred_element_type=_F32)


def _dot_tn(a, b):
    return lax.dot_general(a, b, (((0,), (0,)), ((), ())), preferred_element_type=_F32)


def _rms(x, w):
    return x * lax.rsqrt(jnp.mean(x * x, axis=-1, keepdims=True) + RMS_EPS) * w


def _softplus(x):
    return jnp.maximum(x, 0.0) + jnp.log1p(jnp.exp(-jnp.abs(x)))


def _silu(x):
    half = 0.5 * x
    return half + half * jnp.tanh(half)


def _params(*sem, vmem_limit=VMEM_LIMIT):
    return pltpu.CompilerParams(dimension_semantics=sem, vmem_limit_bytes=vmem_limit)


def _ffn_kernel(x_ref, nw_ref, wg_ref, wu_ref, wd_ref, o_ref, n_ref):
    @pl.when(pl.program_id(1) == 0)
    def _():
        x = x_ref[...]
        n_ref[...] = _rms(x, nw_ref[...]).astype(_BF16)
        o_ref[...] = x

    n = n_ref[...]
    g = _dot(n, wg_ref[...])
    u = _dot(n, wu_ref[...])
    a = (_silu(g) * (0.5 * u)).astype(_BF16)
    o_ref[...] += _dot(a, wd_ref[...].astype(_BF16))


def _ffn(h, norm_w, w_gu, w_down, *, tm, tf):
    m, d = h.shape
    f = w_down.shape[0]
    nf = f // tf
    return pl.pallas_call(
        _ffn_kernel,
        grid=(m // tm, nf),
        in_specs=[
            pl.BlockSpec((tm, d), lambda i, j: (i, 0)),
            pl.BlockSpec((1, d), lambda i, j: (0, 0)),
            pl.BlockSpec((d, tf), lambda i, j: (0, j)),
            pl.BlockSpec((d, tf), lambda i, j: (0, nf + j)),
            pl.BlockSpec((tf, d), lambda i, j: (j, 0)),
        ],
        out_specs=pl.BlockSpec((tm, d), lambda i, j: (i, 0)),
        out_shape=jax.ShapeDtypeStruct((m, d), _F32),
        scratch_shapes=[pltpu.VMEM((tm, d), _BF16)],
        compiler_params=_params("parallel", "arbitrary", vmem_limit=FFN_VMEM_LIMIT),
        name="ffn",
    )(h, norm_w, w_gu, w_gu, w_down)


def _inproj_kernel(na, x_ref, nw_ref, wa_ref, wb_ref, wg_ref, p_ref, gate_ref, n_ref):
    @pl.when(pl.program_id(1) == 0)
    def _():
        n = _rms(x_ref[...], nw_ref[...])
        n_hi = n.astype(_BF16)
        n_lo = (n - n_hi.astype(_F32)).astype(_BF16)
        n_ref[...] = n_hi
        s = _dot_nt(n_hi, wg_ref[...]) + _dot_nt(n_lo, wg_ref[...])
        lane = lax.broadcasted_iota(jnp.int32, s.shape, 1)
        folded = s + pltpu.roll(s, GATE_LANES - 2 * DN_HEADS, axis=1)
        gate_ref[...] = jnp.where(lane < 2 * DN_HEADS, folded, 0.0)

    def project(w_ref):
        res = _dot_nt(n_ref[...], w_ref[...]).astype(_BF16)
        for c in range(p_ref.shape[0]):
            p_ref[c] = res[:, c * HEAD_DIM:(c + 1) * HEAD_DIM]

    @pl.when(pl.program_id(1) < na)
    def _():
        project(wa_ref)

    @pl.when(pl.program_id(1) >= na)
    def _():
        project(wb_ref)


def _inproj(h, norm_w, w_a, w_b, w_gate, *, tm, tn):
    m, d = h.shape
    na, nb = w_a.shape[0] // tn, w_b.shape[0] // tn
    n_main = w_a.shape[0] + w_b.shape[0]
    hpt = tn // HEAD_DIM
    return pl.pallas_call(
        functools.partial(_inproj_kernel, na),
        grid=(m // tm, na + nb),
        in_specs=[
            pl.BlockSpec((tm, d), lambda i, j: (i, 0)),
            pl.BlockSpec((1, d), lambda i, j: (0, 0)),
            pl.BlockSpec((tn, d), lambda i, j: (jnp.minimum(j, na - 1), 0)),
            pl.BlockSpec((tn, d), lambda i, j: (jnp.where(j < na, nb - 1, j - na), 0)),
            pl.BlockSpec((GATE_LANES, d), lambda i, j: (0, 0)),
        ],
        out_specs=[
            pl.BlockSpec((hpt, tm, HEAD_DIM), lambda i, j: (j, i, 0)),
            pl.BlockSpec((tm, GATE_LANES), lambda i, j: (i, 0)),
        ],
        out_shape=[
            jax.ShapeDtypeStruct((n_main // HEAD_DIM, m, HEAD_DIM), _BF16),
            jax.ShapeDtypeStruct((m, GATE_LANES), _F32),
        ],
        scratch_shapes=[pltpu.VMEM((tm, d), _BF16)],
        compiler_params=_params("parallel", "arbitrary"),
        name="in_proj",
    )(h, norm_w, w_a, w_b, w_gate)


def _dn_kernel(q_ref, k_ref, v_ref, z_ref, gate_ref, cq_ref, ck_ref, cv_ref, gpar_ref, onorm_ref,
               o_ref,
               xp_ref, qn_ref, kn_ref, vc_ref, gcb_ref, bb_ref, cs_ref, gsel_ref,
               kq_ref, no_ref, egl_ref):
    nh, t, hd = q_ref.shape
    c = DN_CHUNK
    gr = DN_GROUP_ROWS
    cr = DN_CHUNK_ROWS
    head0 = pl.program_id(1) * nh

    gates = gate_ref[...]
    gpar = gpar_ref[...]
    x = -jnp.exp(gpar[0:1, :]) * _softplus(gates + gpar[1:2, :])
    pos = lax.broadcasted_iota(jnp.int32, (t, GATE_LANES), 0) & (c - 1)
    cs_ref[0:CUMSUM_PAD, :] = jnp.zeros((CUMSUM_PAD, GATE_LANES), _F32)
    s = 1
    while s < c:
        cs_ref[CUMSUM_PAD:CUMSUM_PAD + t, :] = x
        x = x + jnp.where(pos >= s, cs_ref[CUMSUM_PAD - s:CUMSUM_PAD - s + t, :], 0.0)
        s *= 2
    lane = lax.broadcasted_iota(jnp.int32, (t, GATE_LANES), 1)
    comb = jnp.where(lane < DN_HEADS, x, jax.nn.sigmoid(gates))
    comb_hi = comb.astype(_BF16)
    gsel_ref[:, :GATE_LANES] = comb_hi
    gsel_ref[:, GATE_LANES:] = (comb - comb_hi.astype(_F32)).astype(_BF16)
    xp_ref[0:CONV_PAD, :] = jnp.zeros((CONV_PAD, hd), _F32)

    row = lax.broadcasted_iota(jnp.int32, (gr, gr), 0)
    col = lax.broadcasted_iota(jnp.int32, (gr, gr), 1)
    same = (row // c) == (col // c)
    eye = row == col
    lower_incl = same & (row >= col)
    strict = same & (row > col)
    same16 = jnp.where(same, 1.0, 0.0).astype(_BF16)
    eye_c = jnp.where(lax.broadcasted_iota(jnp.int32, (c, gr), 0)
                      == (lax.broadcasted_iota(jnp.int32, (c, gr), 1) & (c - 1)), 1.0, 0.0)

    def conv_silu(x_ref, w):
        xp_ref[CONV_PAD:CONV_PAD + t, :] = x_ref[...].astype(_F32)
        acc = w[CONV_K - 1:CONV_K, :] * xp_ref[CONV_PAD:CONV_PAD + t, :]
        for sh in range(1, CONV_K):
            acc = acc + w[CONV_K - 1 - sh:CONV_K - sh, :] * xp_ref[CONV_PAD - sh:CONV_PAD - sh + t, :]
        return _silu(acc)

    def l2n(y, scale):
        return y * (lax.rsqrt(jnp.sum(y * y, axis=-1, keepdims=True) + L2_EPS) * scale)

    def solve_groups(rows):
        n = range(len(rows))
        q = [qn_ref[pl.ds(r, gr), :] for r in rows]
        k = [kn_ref[pl.ds(r, gr), :] for r in rows]
        v = [vc_ref[pl.ds(r, gr), :] for r in rows]
        bb = [bb_ref[pl.ds(r, gr), :] for r in rows]
        gcb = [gcb_ref[pl.ds(r, gr), :] for r in rows]
        k16 = [k[i].astype(_BF16) for i in n]
        kb = [k[i] * bb[i] for i in n]
        kk = [_dot_nt(kb[i].astype(_BF16), k16[i]) for i in n]
        qk = [_dot_nt(q[i].astype(_BF16), k16[i]) for i in n]
        eg = [jnp.exp(gcb[i]) for i in n]
        lmat, attn, sol = [], [], []
        for i in n:
            gc_sq = jnp.concatenate([gcb[i]] * (gr // hd), axis=-1)
            gc_row = jnp.sum(jnp.where(eye, gc_sq, 0.0), axis=0, keepdims=True)
            decay = jnp.exp(jnp.where(lower_incl, gc_sq - gc_row, -jnp.inf))
            lmat.append(jnp.where(strict, kk[i] * decay, 0.0))
            attn.append((qk[i] * decay).astype(_BF16))
            sol.append(jnp.concatenate([v[i] * bb[i], kb[i] * eg[i]], axis=-1).astype(_BF16))

        def compact(mat):
            out = mat[0:c]
            for j in range(1, DN_GROUP):
                out = out + mat[j * c:(j + 1) * c]
            return out

        def blockdiag(mat):
            return jnp.concatenate([mat.astype(_BF16)] * DN_GROUP, axis=0) * same16

        pw = [compact(lmat[i]) for i in n]
        tinv = [eye_c - pw[i] for i in n]
        bd = [lmat[i].astype(_BF16) for i in n]
        m = 2
        while m < c:
            pw = [_dot(pw[i].astype(_BF16), bd[i]) for i in n]
            bd = [blockdiag(pw[i]) for i in n]
            tinv = [tinv[i] + _dot(tinv[i].astype(_BF16), bd[i]) for i in n]
            m *= 2
        sol = [_dot(blockdiag(tinv[i]), sol[i]).astype(_BF16) for i in n]
        auw = [_dot(attn[i], sol[i]) for i in n]
        kd = []
        for i in n:
            g_last = jnp.concatenate(
                [jnp.broadcast_to(gcb[i][(j + 1) * c - 1:(j + 1) * c, :], (c, hd)) for j in range(DN_GROUP)], axis=0)
            kd.append((k[i] * jnp.exp(g_last - gcb[i])).astype(_BF16))
        kuw = [[_dot_tn(kd[i][j * c:(j + 1) * c], sol[i][j * c:(j + 1) * c]) for j in range(DN_GROUP)] for i in n]
        res = []
        for i in n:
            qp16 = (q[i] * eg[i] - auw[i][:, hd:]).astype(_BF16)
            o0_16 = auw[i][:, :hd].astype(_BF16)
            res.append((qp16, o0_16, [m_.astype(_BF16) for m_ in kuw[i]], eg[i]))
        return res

    def store_group(hl, r, qp16, o0_16, kuw16, eg):
        for j in range(DN_GROUP):
            lo, hi = j * c, (j + 1) * c
            base = pl.multiple_of((r + lo) // c * cr, cr)
            kq_ref[hl, pl.ds(base, hd), :] = kuw16[j][:, hd:]
            kq_ref[hl, pl.ds(base + hd, c), :] = qp16[lo:hi]
            no_ref[hl, pl.ds(base, hd), :] = kuw16[j][:, :hd]
            no_ref[hl, pl.ds(base + hd, c), :] = o0_16[lo:hi]
            r8 = pl.multiple_of((r + lo) // (c // 8), 8)
            egl_ref[hl, pl.ds(r8, 8), :] = jnp.broadcast_to(eg[hi - 1:hi, :], (8, hd))

    def head_prologue(hl, carry):
        head = head0 + hl
        krow = lax.broadcasted_iota(jnp.int32, (2 * GATE_LANES, 2 * hd), 0) & (GATE_LANES - 1)
        ncol = lax.broadcasted_iota(jnp.int32, (2 * GATE_LANES, 2 * hd), 1)
        sel = jnp.where(krow == jnp.where(ncol < hd, head, head + DN_HEADS), 1.0, 0.0).astype(_BF16)
        picked = _dot(gsel_ref[...], sel)
        gcb_ref[...] = picked[:, :hd]
        bb_ref[...] = picked[:, hd:]
        qn_ref[...] = l2n(conv_silu(q_ref.at[hl], cq_ref[hl]), hd ** -0.5)
        kn_ref[...] = l2n(conv_silu(k_ref.at[hl], ck_ref[hl]), 1.0)
        vc_ref[...] = conv_silu(v_ref.at[hl], cv_ref[hl])

        def groups(gi, cc):
            rows = [pl.multiple_of((gi * DN_GROUPS_PER_ITER + sub) * gr, gr) for sub in range(DN_GROUPS_PER_ITER)]
            solved = solve_groups(rows)
            for r, res in zip(rows, solved):
                store_group(hl, r, *res)
            return cc

        lax.fori_loop(0, t // (gr * DN_GROUPS_PER_ITER), groups, 0)
        return carry

    lax.fori_loop(0, nh, head_prologue, 0, unroll=2)

    onorm = onorm_ref[...]

    def chunk(ci, states):
        r = pl.multiple_of(ci * c, c)
        rc = pl.multiple_of(ci * cr, cr)
        r8 = pl.multiple_of(ci * 8, 8)
        heads = range(nh)
        kq = [kq_ref[hl, pl.ds(rc, cr), :] for hl in heads]
        no = [no_ref[hl, pl.ds(rc, cr), :] for hl in heads]
        e_last = [egl_ref[hl, pl.ds(r8, 8), :][0:1] for hl in heads]
        z = [z_ref[hl, pl.ds(r, c), :] for hl in heads]
        ks = [_dot(kq[hl], states[hl].astype(_BF16)) for hl in heads]
        new_states = tuple(states[hl] * e_last[hl] - ks[hl][:hd] + no[hl][:hd].astype(_F32) for hl in heads)
        outs = []
        for hl in heads:
            o = ks[hl][hd:] + no[hl][hd:].astype(_F32)
            y = o * lax.rsqrt(jnp.mean(o * o, axis=-1, keepdims=True) + RMS_EPS) * onorm
            outs.append((y * _silu(z[hl].astype(_F32))).astype(_BF16))
        for hl in heads:
            o_ref[pl.ds(r, c), hl * hd:(hl + 1) * hd] = outs[hl]
        return new_states

    lax.fori_loop(0, t // c, chunk, tuple(jnp.zeros((hd, hd), _F32) for _ in range(nh)), unroll=8)


def _deltanet(p_heads, gates, conv_w, gate_par, out_norm, *, batch, seq):
    hd = HEAD_DIM
    nh = DN_HEADS_PER_STEP
    ng = DN_HEADS // nh
    c = DN_CHUNK
    assert seq % (DN_GROUP_ROWS * DN_GROUPS_PER_ITER) == 0, seq
    heads = lambda part: pl.BlockSpec((nh, seq, hd), lambda b, g: (part * ng + g, b, 0))
    convw = lambda part: pl.BlockSpec((nh, CONV_K, hd), lambda b, g: (part * ng + g, 0, 0))
    f32_rows = lambda: pltpu.VMEM((seq, hd), _F32)
    return pl.pallas_call(
        _dn_kernel,
        grid=(batch, ng),
        in_specs=[
            heads(0), heads(1), heads(2), heads(3),
            pl.BlockSpec((seq, GATE_LANES), lambda b, g: (b, 0)),
            convw(0), convw(1), convw(2),
            pl.BlockSpec((2, GATE_LANES), lambda b, g: (0, 0)),
            pl.BlockSpec((1, hd), lambda b, g: (0, 0)),
        ],
        out_specs=pl.BlockSpec((seq, nh * hd), lambda b, g: (b, g)),
        out_shape=jax.ShapeDtypeStruct((batch * seq, DN_HEADS * hd), _BF16),
        scratch_shapes=[
            pltpu.VMEM((CONV_PAD + seq, hd), _F32),
            f32_rows(), f32_rows(), f32_rows(),
            f32_rows(), f32_rows(),
            pltpu.VMEM((CUMSUM_PAD + seq, GATE_LANES), _F32),
            pltpu.VMEM((seq, 2 * GATE_LANES), _BF16),
            pltpu.VMEM((nh, seq // c * DN_CHUNK_ROWS, hd), _BF16),
            pltpu.VMEM((nh, seq // c * DN_CHUNK_ROWS, hd), _BF16),
            pltpu.VMEM((nh, seq // c * 8, hd), _F32),
        ],
        compiler_params=_params("parallel", "arbitrary"),
        name="deltanet",
    )(p_heads, p_heads, p_heads, p_heads, gates, conv_w, conv_w, conv_w, gate_par, out_norm)


def _sb_kernel(q_ref, k_ref, v_ref, o_ref, acc_ref, carry_ref):
    nh, tq, hd = q_ref.shape
    tk = SB_TK
    qi = pl.program_id(2)
    row = lax.broadcasted_iota(jnp.int32, (tq, tk), 0)
    col = lax.broadcasted_iota(jnp.int32, (tq, tk), 1)
    causal = col < row
    ntri = jnp.where(row >= col, -1.0, 0.0).astype(_BF16)

    def tiles(kt, nt, masked):
        heads = range(nh)
        steps = range(nt)
        ks = [pl.multiple_of((kt - j) * tk, tk) for j in steps]
        k_ = [[k_ref[h, pl.ds(ks[j], tk), :] for h in heads] for j in steps]
        z2 = [[_dot_nt(q_ref[h], k_[j][h]) for h in heads] for j in steps]
        revs = []
        for j in steps:
            revs.append([])
            for h in heads:
                x = jnp.maximum(z2[j][h], 0.0) + jnp.log2(1.0 + jnp.exp2(-jnp.abs(z2[j][h])))
                if masked:
                    x = jnp.where(causal, x, 0.0)
                revs[j].append(_dot(x.astype(_BF16), ntri))
        out = []
        for h in heads:
            acc, carry = acc_ref[h], carry_ref[h]
            for j in steps:
                a = jnp.exp2(z2[j][h] + revs[j][h] + jnp.concatenate([carry] * (tk // hd), axis=-1))
                if masked:
                    a = jnp.where(causal, a, 0.0)
                acc = acc + _dot(a.astype(_BF16), v_ref[h, pl.ds(ks[j], tk), :])
                carry = carry + jnp.broadcast_to(revs[j][h][:, 0:1], (tq, hd))
            out.append((acc, carry))
        for h in heads:
            acc_ref[h], carry_ref[h] = out[h]

    def loop(count, body):
        lax.fori_loop(0, count, lambda i, c: (body(i), c)[1], 0)

    acc_ref[...] = jnp.zeros(acc_ref.shape, _F32)
    carry_ref[...] = jnp.zeros(carry_ref.shape, _F32)
    tiles(qi, 1, True)
    odd = qi % 2
    loop(odd, lambda i: tiles(qi - 1, 1, False))
    loop(qi // 2, lambda i: tiles(qi - 1 - odd - 2 * i, 2, False))
    for h in range(nh):
        o_ref[:, h * hd:(h + 1) * hd] = acc_ref[h].astype(_BF16)


def _stickbreak(p_heads, *, batch, seq, head_off):
    hd = HEAD_DIM
    nh = SB_HEADS_PER_STEP
    ng = SB_HEADS // nh
    nq = seq // SB_TQ
    assert SB_TQ == SB_TK and seq % SB_TQ == 0, seq
    base = head_off // nh
    return pl.pallas_call(
        _sb_kernel,
        grid=(batch, ng, nq),
        in_specs=[
            pl.BlockSpec((nh, SB_TQ, hd), lambda b, g, i: (base + g, b * nq + i, 0)),
            pl.BlockSpec((nh, seq, hd), lambda b, g, i: (base + ng + g, b, 0)),
            pl.BlockSpec((nh, seq, hd), lambda b, g, i: (base + 2 * ng + g, b, 0)),
        ],
        out_specs=pl.BlockSpec((SB_TQ, nh * hd), lambda b, g, i: (b * nq + i, g)),
        out_shape=jax.ShapeDtypeStruct((batch * seq, SB_HEADS * hd), _BF16),
        scratch_shapes=[
            pltpu.VMEM((nh, SB_TQ, hd), _F32),
            pltpu.VMEM((nh, SB_TQ, hd), _F32),
        ],
        compiler_params=_params("parallel", "parallel", "arbitrary"),
        name="stickbreak",
    )(p_heads, p_heads, p_heads)


def _outproj_kernel(h_ref, a_ref, b_ref, wa_ref, wb_ref, o_ref):
    o_ref[...] = h_ref[...] + _dot(a_ref[...], wa_ref[...]) + _dot(b_ref[...], wb_ref[...])


def _outproj(h, o_dn, o_sb, w_out, *, tm):
    m, d = h.shape
    ka = o_dn.shape[1]
    kb = o_sb.shape[1]
    assert ka == kb
    return pl.pallas_call(
        _outproj_kernel,
        grid=(m // tm,),
        in_specs=[
            pl.BlockSpec((tm, d), lambda i: (i, 0)),
            pl.BlockSpec((tm, ka), lambda i: (i, 0)),
            pl.BlockSpec((tm, kb), lambda i: (i, 0)),
            pl.BlockSpec((ka, d), lambda i: (0, 0)),
            pl.BlockSpec((kb, d), lambda i: (1, 0)),
        ],
        out_specs=pl.BlockSpec((tm, d), lambda i: (i, 0)),
        out_shape=jax.ShapeDtypeStruct((m, d), _F32),
        compiler_params=_params("parallel"),
        name="out_proj",
    )(h, o_dn, o_sb, w_out, w_out)


def _ple_kernel(h_ref, p_ref, nw_ref, wg_ref, wp_ref, fw_ref, o_ref):
    rows = h_ref.shape[0] // 2
    parts = [pl.ds(i * rows, rows) for i in range(2)]
    hs = [h_ref[r, :] for r in parts]
    ns = [_rms(h, nw_ref[...]).astype(_BF16) for h in hs]
    gs = [_dot(n, wg_ref[...]) for n in ns]
    es = [_dot(p_ref[r, :].astype(_BF16), wp_ref[...]) for r in parts]
    for r, h, g, e in zip(parts, hs, gs, es):
        o_ref[r, :] = _rms(h + jax.nn.sigmoid(g) * e, fw_ref[...])


def _ple(h, p, norm_w, w_gate, w_proj, final_w, *, tm):
    m, d = h.shape
    pd = p.shape[1]
    return pl.pallas_call(
        _ple_kernel,
        grid=(m // tm,),
        in_specs=[
            pl.BlockSpec((tm, d), lambda i: (i, 0)),
            pl.BlockSpec((tm, pd), lambda i: (i, 0)),
            pl.BlockSpec((1, d), lambda i: (0, 0)),
            pl.BlockSpec((d, d), lambda i: (0, 0)),
            pl.BlockSpec((pd, d), lambda i: (0, 0)),
            pl.BlockSpec((1, d), lambda i: (0, 0)),
        ],
        out_specs=pl.BlockSpec((tm, d), lambda i: (i, 0)),
        out_shape=jax.ShapeDtypeStruct((m, d), _F32),
        compiler_params=_params("parallel"),
        name="ple_final",
    )(h, p, norm_w, w_gate, w_proj, final_w)


def _pad_lanes(v, width):
    return jnp.pad(v.astype(_F32), (0, width - v.shape[0]))


def kernel(x, p, ffn1_norm, ffn1_w_gu, ffn1_w_down, mix_norm, w_in, dn_conv, dn_a_log, dn_dt_bias, dn_out_norm, w_out, ffn2_norm, ffn2_w_gu, ffn2_w_down, ple_norm, ple_w_gate, ple_w_proj, final_norm):
    batch, seq, d = x.shape
    depth = p.shape[0]
    m = batch * seq
    dn_w = DN_HEADS * HEAD_DIM
    gate_lo = 4 * dn_w
    gate_hi = gate_lo + 2 * DN_HEADS
    row = lambda v: v.reshape(1, -1).astype(_F32)

    h = x.reshape(m, d)
    for i in range(depth):
        h = _ffn(h, row(ffn1_norm[i]), ffn1_w_gu[i].astype(_BF16), ffn1_w_down[i].astype(_F32),
                 tm=1024, tf=512)

        wt = jnp.swapaxes(w_in[i], 0, 1)
        sb_w = SB_HEADS * HEAD_DIM
        sb_scale = jnp.where(jnp.arange(3 * sb_w) < sb_w, (HEAD_DIM ** -0.5) * LOG2E, 1.0).astype(_F32)
        w_dn = wt[:gate_lo].astype(_BF16)
        w_sb = (wt[gate_hi:] * sb_scale[:, None]).astype(_BF16)
        w_gate = wt[gate_lo:gate_hi]
        w_gate_hi = w_gate.astype(_BF16)
        w_gate_lo = (w_gate - w_gate_hi.astype(_F32)).astype(_BF16)
        w_gate = jnp.pad(jnp.concatenate([w_gate_hi, w_gate_lo], axis=0), ((0, GATE_LANES - 4 * DN_HEADS), (0, 0)))
        p_heads, gates = _inproj(h, row(mix_norm[i]), w_dn, w_sb, w_gate, tm=1024, tn=1024)

        gate_par = jnp.stack([_pad_lanes(dn_a_log[i], GATE_LANES), _pad_lanes(dn_dt_bias[i], GATE_LANES)])
        conv_w = dn_conv[i].astype(_F32).reshape(CONV_K, 3 * DN_HEADS, HEAD_DIM).transpose(1, 0, 2)
        o_dn = _deltanet(p_heads, gates, conv_w, gate_par, row(dn_out_norm[i]), batch=batch, seq=seq)
        o_sb = _stickbreak(p_heads, batch=batch, seq=seq, head_off=4 * DN_HEADS)
        h = _outproj(h, o_dn, o_sb, w_out[i].astype(_BF16), tm=512)

        h = _ffn(h, row(ffn2_norm[i]), ffn2_w_gu[i].astype(_BF16), ffn2_w_down[i].astype(_F32),
                 tm=1024, tf=512)
        assert i == depth - 1, "the final RMSNorm is fused into the last layer's embedding kernel"
        h = _ple(h, p[i].reshape(m, -1), row(ple_norm[i]), ple_w_gate[i].astype(_BF16),
                 ple_w_proj[i].astype(_BF16), row(final_norm), tm=512)
    return h.reshape(batch, seq, d)
```
